```python
import functools
import jax, jax.numpy as jnp
from jax import lax
import numpy as np

D_MODEL = 4096
BATCH = 1
SEQ = 8192
DEPTH = 1
DEC_BATCH = 16
DEC_SEQ = 32
PAST_LEN = 4096

CHUNK = 64
N_HEADS_A = 16
HEAD_DIM_A = 128
WIDTH_A = N_HEADS_A * HEAD_DIM_A
Q_BLOCK = 128
FORGET_BIAS_INIT = 3.0
GM_GROUPS = 8
GM_WIDTH = 2048
GM_GROUP_DIM = GM_WIDTH // GM_GROUPS
GM_CHUNK = 128
N_EXPERTS = 64
TOP_K = 8
N_EXPERT_GROUPS = 8
TOPK_GROUPS = 4
D_EXPERT = 1024
D_SHARED = 1024
ROUTED_SCALE = 2.5
MOE_BLOCK = 128
RMS_EPS = 1e-6
LN_EPS = 1e-5
COL_Q = 0
COL_K = COL_Q + WIDTH_A
COL_V = COL_K + WIDTH_A
COL_F = COL_V + WIDTH_A
COL_U = COL_F + N_HEADS_A
COL_VB = COL_U + GM_WIDTH
COL_GA = COL_VB + GM_WIDTH
COL_GB = COL_GA + D_MODEL
IN_COLS = COL_GB + D_MODEL

kernel_name = 'fox_gmlp_moe_adaln_stream_step'


def rmsnorm(x, g):
    xf = x.astype(jnp.float32)
    y = xf * lax.rsqrt(jnp.mean(xf * xf, axis=-1, keepdims=True) + RMS_EPS)
    return (y * g.astype(jnp.float32)).astype(x.dtype)


def layernorm(x, g, b):
    xf = x.astype(jnp.float32)
    mu = jnp.mean(xf, axis=-1, keepdims=True)
    var = jnp.mean(jnp.square(xf - mu), axis=-1, keepdims=True)
    y = (xf - mu) * lax.rsqrt(var + LN_EPS) * g.astype(jnp.float32) + b.astype(jnp.float32)
    return y.astype(x.dtype)


def ada_modulation(c, w, b):
    mod = jax.nn.silu(c) @ w + b
    return jnp.split(mod[:, None, :], 6, axis=-1)


def mixer_project(h, w_in, b_forget, ln_g, ln_b):
    bsz, L, _ = h.shape
    proj = h @ w_in
    q = proj[..., COL_Q:COL_K].reshape(bsz, L, N_HEADS_A, HEAD_DIM_A)
    k = proj[..., COL_K:COL_V].reshape(bsz, L, N_HEADS_A, HEAD_DIM_A)
    v = proj[..., COL_V:COL_F].reshape(bsz, L, N_HEADS_A, HEAD_DIM_A)
    logf = jax.nn.log_sigmoid((proj[..., COL_F:COL_U] + b_forget).astype(jnp.float32))
    u = jax.nn.gelu(proj[..., COL_U:COL_VB], approximate=False)
    vb = layernorm(jax.nn.gelu(proj[..., COL_VB:COL_GA], approximate=False), ln_g, ln_b)
    gate_a = jax.nn.sigmoid(proj[..., COL_GA:COL_GB])
    gate_b = jax.nn.sigmoid(proj[..., COL_GB:IN_COLS])
    return q, k, v, logf, u, vb, gate_a, gate_b


def fox_attend(q, cum_q, q_pos, k, v, cum_k, k_pos):
    s = jnp.einsum('bqhd,bkhd->bhqk', q, k).astype(jnp.float32) * (HEAD_DIM_A ** -0.5)
    s = s + jnp.transpose(cum_q, (0, 2, 1))[..., :, None] - jnp.transpose(cum_k, (0, 2, 1))[..., None, :]
    s = jnp.where(k_pos[None, :] <= q_pos[:, None], s, -jnp.inf)
    p = jax.nn.softmax(s, axis=-1)
    return jnp.einsum('bhqk,bkhd->bqhd', p.astype(v.dtype), v)


def fox_prompt(q, k, v, logf):
    bsz, L, H, Dh = q.shape
    nb = L // Q_BLOCK
    cum = jnp.cumsum(logf, axis=1)
    pos = jnp.arange(L, dtype=jnp.int32)
    q_blocks = jnp.moveaxis(q.reshape(bsz, nb, Q_BLOCK, H, Dh), 1, 0)
    c_blocks = jnp.moveaxis(cum.reshape(bsz, nb, Q_BLOCK, H), 1, 0)
    p_blocks = pos.reshape(nb, Q_BLOCK)
    out = lax.map(lambda a: fox_attend(a[0], a[1], a[2], k, v, cum, pos), (q_blocks, c_blocks, p_blocks))
    return jnp.moveaxis(out, 0, 1).reshape(bsz, L, H, Dh)


def fox_sample(q, k, v, logf, cache_k, cache_v, cache_logf):
    T = q.shape[1]
    P = cache_k.shape[1]
    k_all = jnp.concatenate([cache_k.astype(k.dtype), k], axis=1)
    v_all = jnp.concatenate([cache_v.astype(v.dtype), v], axis=1)
    cum = jnp.cumsum(jnp.concatenate([cache_logf.astype(jnp.float32), logf], axis=1), axis=1)
    k_pos = jnp.arange(P + T, dtype=jnp.int32)
    return fox_attend(q, cum[:, P:], k_pos[P:], k_all, v_all, cum, k_pos)


def spatial_mix(u, vb, w_s, b_s):
    bsz, L, _ = vb.shape
    Lc = min(L, GM_CHUNK)
    n = L // Lc
    vg = vb.reshape(bsz, n, Lc, GM_GROUPS, GM_GROUP_DIM)
    w = jnp.tril(w_s[:, :Lc, :Lc])
    mixed = jnp.einsum('gij,bnjgc->bnigc', w, vg) + jnp.transpose(b_s[:, :Lc])[None, None, :, :, None]
    return u * mixed.reshape(bsz, L, GM_WIDTH)


def merge_branches(attn, spatial, gate_a, gate_b, w_a, w_b, w_out):
    bsz, L = attn.shape[:2]
    oa = attn.reshape(bsz, L, WIDTH_A) @ w_a
    ob = spatial @ w_b
    return (gate_a * oa + gate_b * ob) @ w_out


def route(xs, w_router, router_bias):
    scores = jax.nn.sigmoid((xs @ w_router).astype(jnp.float32))
    sel = scores + router_bias.astype(jnp.float32)
    per_group = N_EXPERTS // N_EXPERT_GROUPS
    grp = sel.reshape(-1, N_EXPERT_GROUPS, per_group)
    grp_score = jnp.sum(lax.top_k(grp, 2)[0], axis=-1)
    top_grp = lax.top_k(grp_score, TOPK_GROUPS)[1]
    grp_mask = jnp.sum(jax.nn.one_hot(top_grp, N_EXPERT_GROUPS, dtype=jnp.float32), axis=-2) > 0
    sel = jnp.where(jnp.repeat(grp_mask, per_group, axis=-1), sel, -jnp.inf)
    idx = lax.top_k(sel, TOP_K)[1]
    w = jnp.take_along_axis(scores, idx, axis=-1)
    w = w / jnp.sum(w, axis=-1, keepdims=True) * ROUTED_SCALE
    return idx, w


def swiglu(x, wg, wu, wd):
    return (jax.nn.silu(x @ wg) * (x @ wu)) @ wd


def moe_ffn(h, w_router, router_bias, w_gate, w_up, w_down, ws_gate, ws_up, ws_down):
    bsz, L, D = h.shape
    xs = h.reshape(-1, D)
    N = xs.shape[0]
    idx, gates = route(xs, w_router, router_bias)
    NK = N * TOP_K
    flat_e = idx.reshape(-1)
    counts = jnp.zeros((N_EXPERTS,), jnp.int32).at[flat_e].add(1)
    padded = (counts + MOE_BLOCK - 1) // MOE_BLOCK * MOE_BLOCK
    pad_end = jnp.cumsum(padded)
    pad_start = pad_end - padded
    order = jnp.argsort(flat_e, stable=True)
    sorted_e = flat_e[order]
    rank = jnp.arange(NK, dtype=jnp.int32) - (jnp.cumsum(counts) - counts)[sorted_e]
    slot = jnp.zeros((NK,), jnp.int32).at[order].set(pad_start[sorted_e] + rank)
    n_blocks = -(-(NK + N_EXPERTS * (MOE_BLOCK - 1)) // MOE_BLOCK)
    n_slots = n_blocks * MOE_BLOCK
    slot_tok = jnp.full((n_slots,), N, jnp.int32).at[slot].set(jnp.arange(NK, dtype=jnp.int32) // TOP_K)
    slot_gate = jnp.zeros((n_slots,), gates.dtype).at[slot].set(gates.reshape(-1))
    block_exp = jnp.minimum(
        jnp.searchsorted(pad_end, jnp.arange(n_blocks, dtype=jnp.int32) * MOE_BLOCK, side='right'),
        N_EXPERTS - 1)
    xs_pad = jnp.concatenate([xs, jnp.zeros((1, D), xs.dtype)], axis=0)

    def expert_block(args):
        tok, gate, e = args
        xb = xs_pad[tok]
        hid = jax.nn.silu(xb @ w_gate[e]) * (xb @ w_up[e])
        return (hid * gate[:, None].astype(hid.dtype)) @ w_down[e]

    out = lax.map(expert_block, (slot_tok.reshape(n_blocks, MOE_BLOCK),
                                 slot_gate.reshape(n_blocks, MOE_BLOCK), block_exp))
    routed = jax.ops.segment_sum(out.reshape(n_slots, D), slot_tok, num_segments=N + 1)[:N]
    shared = swiglu(xs, ws_gate, ws_up, ws_down)
    return (routed + shared).reshape(bsz, L, D)


def trunk_layer(x, c, attend, p):
    shift1, scale1, gate1, shift2, scale2, gate2 = ada_modulation(c, p['ada_w'], p['ada_b'])
    h = rmsnorm(x, p['norm1_g']) * (1.0 + scale1) + shift1
    q, k, v, logf, u, vb, ga, gb = mixer_project(h, p['w_in'], p['b_forget'], p['gm_ln_g'], p['gm_ln_b'])
    attn = attend(q, k, v, logf)
    spatial = spatial_mix(u, vb, p['gm_ws'], p['gm_bs'])
    x = x + gate1 * merge_branches(attn, spatial, ga, gb, p['w_branch_a'], p['w_branch_b'], p['w_out'])
    h = rmsnorm(x, p['norm2_g']) * (1.0 + scale2) + shift2
    x = x + gate2 * moe_ffn(h, p['w_router'], p['router_bias'], p['w_exp_gate'], p['w_exp_up'],
                            p['w_exp_down'], p['w_sh_gate'], p['w_sh_up'], p['w_sh_down'])
    return x, k, v, logf, vb


def setup_inputs(seed: int = 0) -> dict:
    key = jax.random.key(seed)
    ks = jax.random.split(key, 32)
    D = D_MODEL
    E = N_EXPERTS

    def nrm(k, shape, scale=1.0):
        return jax.random.normal(k, shape, jnp.float32) * scale

    return {
        'x_prompt': nrm(ks[0], (BATCH, SEQ, D)),
        'x_sample': nrm(ks[1], (DEC_BATCH, DEC_SEQ, D)),
        'cache_k': nrm(ks[2], (DEPTH, DEC_BATCH, PAST_LEN, N_HEADS_A, HEAD_DIM_A)),
        'cache_v': nrm(ks[3], (DEPTH, DEC_BATCH, PAST_LEN, N_HEADS_A, HEAD_DIM_A)),
        'cache_logf': jax.nn.log_sigmoid(FORGET_BIAS_INIT + nrm(ks[4], (DEPTH, DEC_BATCH, PAST_LEN, N_HEADS_A))),
        'c_prompt': nrm(ks[5], (BATCH, D)),
        'c_sample': nrm(ks[6], (DEC_BATCH, D)),
        'ada_w': nrm(ks[7], (DEPTH, D, 6 * D), 0.5 * D ** -0.5),
        'ada_b': nrm(ks[8], (DEPTH, 6 * D), 0.01),
        'norm1_g': 1.0 + nrm(ks[9], (DEPTH, D), 0.05),
        'w_in': nrm(ks[10], (DEPTH, D, IN_COLS), D ** -0.5),
        'b_forget': FORGET_BIAS_INIT + nrm(ks[11], (DEPTH, N_HEADS_A), 0.5),
        'gm_ln_g': 1.0 + nrm(ks[12], (DEPTH, GM_WIDTH), 0.05),
        'gm_ln_b': nrm(ks[13], (DEPTH, GM_WIDTH), 0.01),
        'gm_ws': nrm(ks[14], (DEPTH, GM_GROUPS, GM_CHUNK, GM_CHUNK), GM_CHUNK ** -0.5),
        'gm_bs': 1.0 + nrm(ks[15], (DEPTH, GM_GROUPS, GM_CHUNK), 0.05),
        'w_branch_a': nrm(ks[16], (DEPTH, WIDTH_A, D), WIDTH_A ** -0.5),
        'w_branch_b': nrm(ks[17], (DEPTH, GM_WIDTH, D), GM_WIDTH ** -0.5),
        'w_out': nrm(ks[18], (DEPTH, D, D), D ** -0.5),
        'norm2_g': 1.0 + nrm(ks[19], (DEPTH, D), 0.05),
        'w_router': nrm(ks[20], (DEPTH, D, E), D ** -0.5),
        'router_bias': nrm(ks[21], (DEPTH, E), 0.01),
        'w_exp_gate': nrm(ks[22], (DEPTH, E, D, D_EXPERT), D ** -0.5),
        'w_exp_up': nrm(ks[23], (DEPTH, E, D, D_EXPERT), D ** -0.5),
        'w_exp_down': nrm(ks[24], (DEPTH, E, D_EXPERT, D), D_EXPERT ** -0.5),
        'w_sh_gate': nrm(ks[25], (DEPTH, D, D_SHARED), D ** -0.5),
        'w_sh_up': nrm(ks[26], (DEPTH, D, D_SHARED), D ** -0.5),
        'w_sh_down': nrm(ks[27], (DEPTH, D_SHARED, D), D_SHARED ** -0.5),
        'final_g': 1.0 + nrm(ks[28], (D,), 0.05),
    }


def reference(x_prompt, x_sample, cache_k, cache_v, cache_logf, c_prompt, c_sample,
              ada_w, ada_b, norm1_g, w_in, b_forget, gm_ln_g, gm_ln_b, gm_ws, gm_bs,
              w_branch_a, w_branch_b, w_out, norm2_g, w_router, router_bias,
              w_exp_gate, w_exp_up, w_exp_down, w_sh_gate, w_sh_up, w_sh_down, final_g):
    xp = x_prompt
    xs = x_sample
    kp_l, vp_l, fp_l, ks_l, vs_l, fs_l, gs_l = [], [], [], [], [], [], []
    for l in range(DEPTH):
        p = {
            'ada_w': ada_w[l], 'ada_b': ada_b[l], 'norm1_g': norm1_g[l], 'w_in': w_in[l],
            'b_forget': b_forget[l], 'gm_ln_g': gm_ln_g[l], 'gm_ln_b': gm_ln_b[l],
            'gm_ws': gm_ws[l], 'gm_bs': gm_bs[l], 'w_branch_a': w_branch_a[l],
            'w_branch_b': w_branch_b[l], 'w_out': w_out[l], 'norm2_g': norm2_g[l],
            'w_router': w_router[l], 'router_bias': router_bias[l], 'w_exp_gate': w_exp_gate[l],
            'w_exp_up': w_exp_up[l], 'w_exp_down': w_exp_down[l], 'w_sh_gate': w_sh_gate[l],
            'w_sh_up': w_sh_up[l], 'w_sh_down': w_sh_down[l],
        }
        xp, kp, vp, fp, _ = trunk_layer(xp, c_prompt, fox_prompt, p)
        attend_s = functools.partial(fox_sample, cache_k=cache_k[l], cache_v=cache_v[l], cache_logf=cache_logf[l])
        xs, ks_, vs_, fs_, gs_ = trunk_layer(xs, c_sample, attend_s, p)
        kp_l.append(kp); vp_l.append(vp); fp_l.append(fp)
        ks_l.append(ks_); vs_l.append(vs_); fs_l.append(fs_); gs_l.append(gs_)
    y_prompt = rmsnorm(xp, final_g)
    y_sample = rmsnorm(xs, final_g)
    new_k_prompt = jnp.stack(kp_l)
    new_v_prompt = jnp.stack(vp_l)
    new_logf_prompt = jnp.stack(fp_l)
    new_k_sample = jnp.stack(ks_l)
    new_v_sample = jnp.stack(vs_l)
    new_logf_sample = jnp.stack(fs_l)
    new_gm_v_sample = jnp.stack(gs_l)
    return (y_prompt, y_sample, new_k_prompt, new_v_prompt, new_logf_prompt,
            new_k_sample, new_v_sample, new_logf_sample, new_gm_v_sample)
```

```python
import functools
import math

import jax
import jax.numpy as jnp
import numpy as np
from jax import lax
from jax.experimental import pallas as pl
from jax.experimental.pallas import tpu as pltpu

F32 = jnp.float32
BF16 = jnp.bfloat16
I32 = jnp.int32

N_HEADS = 16
HEAD_DIM = 128
WIDTH_A = N_HEADS * HEAD_DIM
GM_GROUPS = 8
GM_WIDTH = 2048
GM_GROUP_DIM = GM_WIDTH // GM_GROUPS
GM_CHUNK = 128
N_EXPERTS = 64
TOP_K = 8
N_EXPERT_GROUPS = 8
TOPK_GROUPS = 4
ROUTED_SCALE = 2.5
RMS_EPS = 1e-6
LN_EPS = 1e-5

LANES = 128
NORM_TM = 256
MOE_TM = 256
COMBINE_TB = 32
DISPATCH_CH = 64
VMEM_LIMIT = 56 << 20


def _cparams(sem, vmem=VMEM_LIMIT):
    return pltpu.CompilerParams(dimension_semantics=sem, vmem_limit_bytes=vmem)


def _mm(name, xs, ws, aux, epilogue, out_dtypes, *, tm, tn, x_pre=None):
    M = xs[0].shape[0]
    N = ws[0][1].shape[1]
    assert M % tm == 0 and N % tn == 0, (name, M, N, tm, tn)
    in_specs, args = [], []
    for x in xs:
        in_specs.append(pl.BlockSpec((tm, x.shape[1]), lambda i, j: (i, 0)))
        args.append(x)
    for _, w in ws:
        in_specs.append(pl.BlockSpec((w.shape[0], tn), lambda i, j: (0, j)))
        args.append(w)
    for a in aux:
        if a.shape[0] == 1:
            in_specs.append(pl.BlockSpec((1, tn), lambda i, j: (0, j)))
        else:
            assert a.shape == (M, N), (name, a.shape)
            in_specs.append(pl.BlockSpec((tm, tn), lambda i, j: (i, j)))
        args.append(a)
    n_x, n_w, n_a = len(xs), len(ws), len(aux)
    x_of_w = [xi for xi, _ in ws]

    def body(*refs):
        x_refs = refs[:n_x]
        w_refs = refs[n_x:n_x + n_w]
        a_refs = refs[n_x + n_w:n_x + n_w + n_a]
        o_refs = refs[n_x + n_w + n_a:]
        xv = []
        for r in x_refs:
            x = r[...]
            if x_pre is not None:
                x = x_pre(x)
            xv.append(x.astype(BF16))
        accs = [jnp.dot(xv[xi], w_refs[k][...].astype(BF16), preferred_element_type=F32)
                for k, xi in enumerate(x_of_w)]
        res = epilogue(accs, [r[...].astype(F32) for r in a_refs])
        for o, r in zip(o_refs, res):
            o[...] = r.astype(o.dtype)

    return pl.pallas_call(
        body,
        grid=(M // tm, N // tn),
        in_specs=in_specs,
        out_specs=[pl.BlockSpec((tm, tn), lambda i, j: (i, j)) for _ in out_dtypes],
        out_shape=[jax.ShapeDtypeStruct((M, N), d) for d in out_dtypes],
        compiler_params=_cparams(("parallel", "arbitrary")),
        name=name,
    )(*args)


def _gelu(x):
    return x * (lax.erf(x * (1.0 / math.sqrt(2.0))) + 1.0) * 0.5


def _log_sigmoid(z):
    return jnp.minimum(z, 0.0) - jnp.log1p(jnp.exp(-jnp.abs(z)))


def _silu(x):
    return x * jax.nn.sigmoid(x)


def _aux_spec(a, tm, d):
    if a.shape[0] == 1:
        return pl.BlockSpec((1, d), lambda i: (0, 0))
    return pl.BlockSpec((tm, d), lambda i: (i, 0))


def _normmod_math(x, g, scale, shift):
    y = x * lax.rsqrt(jnp.mean(x * x, axis=-1, keepdims=True) + RMS_EPS)
    return (y * g) * (1.0 + scale) + shift


def _normmod(name, x, g, scale, shift, *, tm, out_dtype):
    M, D = x.shape

    def body(x_ref, g_ref, sc_ref, sh_ref, o_ref):
        o_ref[...] = _normmod_math(x_ref[...], g_ref[...], sc_ref[...], sh_ref[...]).astype(o_ref.dtype)

    return pl.pallas_call(
        body,
        grid=(M // tm,),
        in_specs=[pl.BlockSpec((tm, D), lambda i: (i, 0)), pl.BlockSpec((1, D), lambda i: (0, 0)),
                  _aux_spec(scale, tm, D), _aux_spec(shift, tm, D)],
        out_specs=pl.BlockSpec((tm, D), lambda i: (i, 0)),
        out_shape=jax.ShapeDtypeStruct((M, D), out_dtype),
        compiler_params=_cparams(("parallel",)),
        name=name,
    )(x, g, scale, shift)


def _normmod_router(name, x, g, scale, shift, wr_hi, wr_lo, *, tm):
    M, D = x.shape
    E = wr_hi.shape[0]
    nt = (((1,), (1,)), ((), ()))

    def body(x_ref, g_ref, sc_ref, sh_ref, whi_ref, wlo_ref, h_ref, lg_ref):
        h = _normmod_math(x_ref[...], g_ref[...], sc_ref[...], sh_ref[...])
        h_ref[...] = h
        h_hi = h.astype(BF16)
        h_lo = (h - h_hi.astype(F32)).astype(BF16)
        whi = whi_ref[...]
        lg = lax.dot_general(whi, h_hi, nt, preferred_element_type=F32)
        lg = lg + lax.dot_general(wlo_ref[...], h_hi, nt, preferred_element_type=F32)
        lg = lg + lax.dot_general(whi, h_lo, nt, preferred_element_type=F32)
        lg_ref[...] = lg

    return pl.pallas_call(
        body,
        grid=(M // tm,),
        in_specs=[pl.BlockSpec((tm, D), lambda i: (i, 0)), pl.BlockSpec((1, D), lambda i: (0, 0)),
                  _aux_spec(scale, tm, D), _aux_spec(shift, tm, D),
                  pl.BlockSpec((E, D), lambda i: (0, 0)), pl.BlockSpec((E, D), lambda i: (0, 0))],
        out_specs=[pl.BlockSpec((tm, D), lambda i: (i, 0)), pl.BlockSpec((E, tm), lambda i: (0, i))],
        out_shape=[jax.ShapeDtypeStruct((M, D), F32), jax.ShapeDtypeStruct((E, M), F32)],
        compiler_params=_cparams(("parallel",)),
        name=name,
    )(x, g, scale, shift, wr_hi, wr_lo)


def _scan_lanes(x):
    lane = lax.broadcasted_iota(I32, x.shape, 1)
    d = 1
    while d < LANES:
        x = x + jnp.where(lane >= d, pltpu.roll(x, d, axis=1), 0.0)
        d *= 2
    return x


def _cumsum_lanes(name, x):
    B, R, L = x.shape
    assert L % LANES == 0
    n_chunks = L // LANES

    def body(x_ref, o_ref):
        def step(c, carry):
            off = pl.multiple_of(c * LANES, LANES)
            s = _scan_lanes(x_ref[:, pl.ds(off, LANES)]) + carry
            o_ref[:, pl.ds(off, LANES)] = s
            return s[:, LANES - 1:LANES]
        lax.fori_loop(0, n_chunks, step, jnp.zeros((R, 1), F32))

    return pl.pallas_call(
        body,
        grid=(B,),
        in_specs=[pl.BlockSpec((None, R, L), lambda b: (b, 0, 0))],
        out_specs=pl.BlockSpec((None, R, L), lambda b: (b, 0, 0)),
        out_shape=jax.ShapeDtypeStruct((B, R, L), F32),
        compiler_params=_cparams(("parallel",)),
        name=name,
    )(x)


_NT = (((1,), (1,)), ((), ()))
_SM_SCALE = HEAD_DIM ** -0.5


def _online_update(s, v, m_prev, l_prev, acc_prev):
    m_new = jnp.maximum(m_prev, jnp.max(s, axis=-1, keepdims=True))
    alpha = jnp.exp(m_prev - m_new)
    p = jnp.exp(s - m_new)
    l_new = alpha * l_prev + jnp.sum(p, axis=-1, keepdims=True)
    acc_new = alpha * acc_prev + jnp.dot(p.astype(BF16), v, preferred_element_type=F32)
    return m_new, l_new, acc_new


def _causal(s):
    row = lax.broadcasted_iota(I32, s.shape, 0)
    col = lax.broadcasted_iota(I32, s.shape, 1)
    return jnp.where(col <= row, s, -jnp.inf)


def _fox_prompt(q, k, v, cq, ck, *, tq):
    L = q.shape[0]
    nq = L // tq
    pairs = [(i, j) for i in range(nq) for j in range(i + 1)]
    qi_tab = jnp.asarray(np.array([p[0] for p in pairs], np.int32))
    ki_tab = jnp.asarray(np.array([p[1] for p in pairs], np.int32))

    def body(qi_ref, ki_ref, q_ref, k_ref, v_ref, cq_ref, ck_ref, o_ref, m_sc, l_sc, acc_sc):
        t = pl.program_id(1)
        qb = qi_ref[t]
        kb = ki_ref[t]

        @pl.when(kb == 0)
        def _():
            m_sc[...] = jnp.full(m_sc.shape, -jnp.inf, F32)
            l_sc[...] = jnp.zeros(l_sc.shape, F32)
            acc_sc[...] = jnp.zeros(acc_sc.shape, F32)

        def scores():
            s = lax.dot_general(q_ref[...], k_ref[...], _NT, preferred_element_type=F32)
            return s * _SM_SCALE + (cq_ref[...] - ck_ref[...])

        def update(s):
            m, l, acc = _online_update(s, v_ref[...], m_sc[...], l_sc[...], acc_sc[...])
            m_sc[...] = m
            l_sc[...] = l
            acc_sc[...] = acc

        @pl.when(kb < qb)
        def _():
            update(scores())

        @pl.when(kb == qb)
        def _():
            update(_causal(scores()))
            o_ref[...] = (acc_sc[...] / l_sc[...]).astype(o_ref.dtype)

    return pl.pallas_call(
        body,
        grid_spec=pltpu.PrefetchScalarGridSpec(
            num_scalar_prefetch=2,
            grid=(N_HEADS, len(pairs)),
            in_specs=[
                pl.BlockSpec((tq, HEAD_DIM), lambda h, t, qi, ki: (qi[t], h)),
                pl.BlockSpec((tq, HEAD_DIM), lambda h, t, qi, ki: (ki[t], h)),
                pl.BlockSpec((tq, HEAD_DIM), lambda h, t, qi, ki: (ki[t], h)),
                pl.BlockSpec((None, tq, 1), lambda h, t, qi, ki: (h, qi[t], 0)),
                pl.BlockSpec((None, 1, tq), lambda h, t, qi, ki: (h, 0, ki[t])),
            ],
            out_specs=pl.BlockSpec((tq, HEAD_DIM), lambda h, t, qi, ki: (qi[t], h)),
            scratch_shapes=[pltpu.VMEM((tq, 1), F32), pltpu.VMEM((tq, 1), F32), pltpu.VMEM((tq, HEAD_DIM), F32)],
        ),
        out_shape=jax.ShapeDtypeStruct((L, N_HEADS * HEAD_DIM), BF16),
        compiler_params=_cparams(("parallel", "arbitrary")),
        name="fox_prompt",
    )(qi_tab, ki_tab, q, k, v, cq, ck)


def _fox_sample(q, k_new, v_new, cache_k, cache_v, cq, ck_cache, ck_new, *, tk):
    B, T, W = q.shape
    P = cache_k.shape[1]
    nk = P // tk

    def body(q_ref, kn_ref, vn_ref, ck_ref, cv_ref, cq_ref, cc_ref, cn_ref, o_ref, m_sc, l_sc, acc_sc):
        c = pl.program_id(1)

        @pl.when(c == 0)
        def _():
            for h in range(N_HEADS):
                sl = slice(h * HEAD_DIM, (h + 1) * HEAD_DIM)
                s = lax.dot_general(q_ref[:, sl], kn_ref[:, sl], _NT, preferred_element_type=F32)
                s = _causal(s * _SM_SCALE + (cq_ref[h] - cn_ref[h]))
                m, l, acc = _online_update(s, vn_ref[:, sl], jnp.full((T, 1), -jnp.inf, F32),
                                           jnp.zeros((T, 1), F32), jnp.zeros((T, HEAD_DIM), F32))
                m_sc[h] = m
                l_sc[h] = l
                acc_sc[h] = acc

        for h in range(N_HEADS):
            sl = slice(h * HEAD_DIM, (h + 1) * HEAD_DIM)
            kh = ck_ref[:, sl].astype(BF16)
            vh = cv_ref[:, sl].astype(BF16)
            s = lax.dot_general(q_ref[:, sl], kh, _NT, preferred_element_type=F32)
            s = s * _SM_SCALE + (cq_ref[h] - cc_ref[h])
            m, l, acc = _online_update(s, vh, m_sc[h], l_sc[h], acc_sc[h])
            m_sc[h] = m
            l_sc[h] = l
            acc_sc[h] = acc

        @pl.when(c == nk - 1)
        def _():
            for h in range(N_HEADS):
                sl = slice(h * HEAD_DIM, (h + 1) * HEAD_DIM)
                o_ref[:, sl] = (acc_sc[h] / l_sc[h]).astype(o_ref.dtype)

    return pl.pallas_call(
        body,
        grid=(B, nk),
        in_specs=[
            pl.BlockSpec((None, T, W), lambda b, c: (b, 0, 0)),
            pl.BlockSpec((None, T, W), lambda b, c: (b, 0, 0)),
            pl.BlockSpec((None, T, W), lambda b, c: (b, 0, 0)),
            pl.BlockSpec((None, tk, W), lambda b, c: (b, c, 0)),
            pl.BlockSpec((None, tk, W), lambda b, c: (b, c, 0)),
            pl.BlockSpec((None, N_HEADS, T, 1), lambda b, c: (b, 0, 0, 0)),
            pl.BlockSpec((None, N_HEADS, 1, tk), lambda b, c: (b, 0, 0, c)),
            pl.BlockSpec((None, N_HEADS, 1, T), lambda b, c: (b, 0, 0, 0)),
        ],
        out_specs=pl.BlockSpec((None, T, W), lambda b, c: (b, 0, 0)),
        out_shape=jax.ShapeDtypeStruct((B, T, W), BF16),
        scratch_shapes=[pltpu.VMEM((N_HEADS, T, 1), F32), pltpu.VMEM((N_HEADS, T, 1), F32),
                        pltpu.VMEM((N_HEADS, T, HEAD_DIM), F32)],
        compiler_params=_cparams(("parallel", "arbitrary")),
        name="fox_sample",
    )(q, k_new, v_new, cache_k, cache_v, cq, ck_cache, ck_new)


def _spatial(name, gv, u, ln_g, ln_b, ws, bs, *, lc, tm):
    M = gv.shape[0]
    n_chunks = tm // lc

    def body(gv_ref, u_ref, lg_ref, lb_ref, ws_ref, bs_ref, vb_ref, sp_ref):
        x = gv_ref[...]
        mu = jnp.mean(x, axis=-1, keepdims=True)
        xc = x - mu
        var = jnp.mean(xc * xc, axis=-1, keepdims=True)
        vb = xc * lax.rsqrt(var + LN_EPS) * lg_ref[...] + lb_ref[...]
        vb_ref[...] = vb
        row = lax.broadcasted_iota(I32, (lc, lc), 0)
        col = lax.broadcasted_iota(I32, (lc, lc), 1)
        for g in range(GM_GROUPS):
            w = jnp.where(col <= row, ws_ref[g], 0.0).astype(BF16)
            b = bs_ref[g]
            cs = slice(g * GM_GROUP_DIM, (g + 1) * GM_GROUP_DIM)
            for c in range(n_chunks):
                rs = slice(c * lc, (c + 1) * lc)
                mixed = jnp.dot(w, vb_ref[rs, cs].astype(BF16), preferred_element_type=F32) + b
                sp_ref[rs, cs] = (u_ref[rs, cs] * mixed).astype(sp_ref.dtype)

    return pl.pallas_call(
        body,
        grid=(M // tm,),
        in_specs=[pl.BlockSpec((tm, GM_WIDTH), lambda i: (i, 0)), pl.BlockSpec((tm, GM_WIDTH), lambda i: (i, 0)),
                  pl.BlockSpec((1, GM_WIDTH), lambda i: (0, 0)), pl.BlockSpec((1, GM_WIDTH), lambda i: (0, 0)),
                  pl.BlockSpec((GM_GROUPS, lc, lc), lambda i: (0, 0, 0)),
                  pl.BlockSpec((GM_GROUPS, lc, 1), lambda i: (0, 0, 0))],
        out_specs=[pl.BlockSpec((tm, GM_WIDTH), lambda i: (i, 0)), pl.BlockSpec((tm, GM_WIDTH), lambda i: (i, 0))],
        out_shape=[jax.ShapeDtypeStruct((M, GM_WIDTH), F32), jax.ShapeDtypeStruct((M, GM_WIDTH), BF16)],
        compiler_params=_cparams(("parallel",)),
        name=name,
    )(gv, u, ln_g, ln_b, ws, bs)


def _route(logits_t, bias, *, tt):
    E, M = logits_t.shape
    per_group = E // N_EXPERT_GROUPS
    neg = -jnp.inf

    def body(lg_ref, b_ref, idx_ref, gate_ref, oh_ref):
        scores = jax.nn.sigmoid(lg_ref[...])
        sel = scores + b_ref[...]
        sel3 = sel.reshape(N_EXPERT_GROUPS, per_group, tt)
        io_in = lax.broadcasted_iota(I32, sel3.shape, 1)
        m1 = jnp.max(sel3, axis=1, keepdims=True)
        first = jnp.min(jnp.where(sel3 == m1, io_in, per_group), axis=1, keepdims=True)
        m2 = jnp.max(jnp.where(io_in == first, neg, sel3), axis=1, keepdims=True)
        grp_score = (m1 + m2).reshape(N_EXPERT_GROUPS, tt)
        io_g = lax.broadcasted_iota(I32, grp_score.shape, 0)
        grp_sel = jnp.zeros(grp_score.shape, F32)
        work = grp_score
        for _ in range(TOPK_GROUPS):
            mx = jnp.max(work, axis=0, keepdims=True)
            fi = jnp.min(jnp.where(work == mx, io_g, N_EXPERT_GROUPS), axis=0, keepdims=True)
            pick = io_g == fi
            grp_sel = jnp.where(pick, 1.0, grp_sel)
            work = jnp.where(pick, neg, work)
        keep3 = jnp.broadcast_to(grp_sel.reshape(N_EXPERT_GROUPS, 1, tt), sel3.shape)
        work = jnp.where(keep3 > 0.5, sel3, neg).reshape(E, tt)
        io_e = lax.broadcasted_iota(I32, (E, tt), 0)
        onehot = jnp.zeros((E, tt), F32)
        wts = []
        for k in range(TOP_K):
            mx = jnp.max(work, axis=0, keepdims=True)
            fi = jnp.min(jnp.where(work == mx, io_e, E), axis=0, keepdims=True)
            pick = io_e == fi
            idx_ref[k:k + 1, :] = fi
            wts.append(jnp.sum(jnp.where(pick, scores, 0.0), axis=0, keepdims=True))
            onehot = jnp.where(pick, 1.0, onehot)
            work = jnp.where(pick, neg, work)
        total = wts[0]
        for k in range(1, TOP_K):
            total = total + wts[k]
        for k in range(TOP_K):
            gate_ref[k:k + 1, :] = wts[k] / total * ROUTED_SCALE
        oh_ref[...] = onehot

    return pl.pallas_call(
        body,
        grid=(M // tt,),
        in_specs=[pl.BlockSpec((E, tt), lambda i: (0, i)), pl.BlockSpec((E, 1), lambda i: (0, 0))],
        out_specs=[pl.BlockSpec((TOP_K, tt), lambda i: (0, i)), pl.BlockSpec((TOP_K, tt), lambda i: (0, i)),
                   pl.BlockSpec((E, tt), lambda i: (0, i))],
        out_shape=[jax.ShapeDtypeStruct((TOP_K, M), I32), jax.ShapeDtypeStruct((TOP_K, M), F32),
                   jax.ShapeDtypeStruct((E, M), F32)],
        compiler_params=_cparams(("parallel",)),
        name="moe_route",
    )(logits_t, bias)


def _slots(idx_t, pos_t, start, *, tt):
    K, M = idx_t.shape
    E = pos_t.shape[0]

    def body(idx_ref, pos_ref, st_ref, o_ref):
        base = pos_ref[...] - 1.0 + st_ref[...]
        io_e = lax.broadcasted_iota(I32, (E, tt), 0)
        for k in range(K):
            pick = io_e == idx_ref[k:k + 1, :]
            o_ref[k:k + 1, :] = jnp.sum(jnp.where(pick, base, 0.0), axis=0, keepdims=True).astype(I32)

    return pl.pallas_call(
        body,
        grid=(M // tt,),
        in_specs=[pl.BlockSpec((K, tt), lambda i: (0, i)), pl.BlockSpec((E, tt), lambda i: (0, i)),
                  pl.BlockSpec((E, 1), lambda i: (0, 0))],
        out_specs=pl.BlockSpec((K, tt), lambda i: (0, i)),
        out_shape=jax.ShapeDtypeStruct((K, M), I32),
        compiler_params=_cparams(("parallel",)),
        name="moe_slots",
    )(idx_t, pos_t, start)


def _dispatch(h, slots, zeros):
    M, D = h.shape
    n_chunks = M // DISPATCH_CH
    per_chunk = DISPATCH_CH * TOP_K

    def body(slots_ref, h_ref, z_ref, o_ref, sem):
        del z_ref

        def row_copy(n, s):
            return pltpu.make_async_copy(h_ref.at[pl.ds(n, 1)], o_ref.at[pl.ds(s, 1)], sem)

        def wait_chunk():
            def w(i, c):
                row_copy(0, 0).wait()
                return c
            lax.fori_loop(0, per_chunk, w, 0)

        def chunk(ci, c):
            def tok(i, c2):
                n = ci * DISPATCH_CH + i
                for k in range(TOP_K):
                    row_copy(n, slots_ref[n * TOP_K + k]).start()
                return c2
            lax.fori_loop(0, DISPATCH_CH, tok, 0)

            @pl.when(ci > 0)
            def _():
                wait_chunk()
            return c

        lax.fori_loop(0, n_chunks, chunk, 0)
        wait_chunk()

    return pl.pallas_call(
        body,
        grid_spec=pltpu.PrefetchScalarGridSpec(
            num_scalar_prefetch=1,
            grid=(1,),
            in_specs=[pl.BlockSpec(memory_space=pl.ANY), pl.BlockSpec(memory_space=pl.ANY)],
            out_specs=pl.BlockSpec(memory_space=pl.ANY),
            scratch_shapes=[pltpu.SemaphoreType.DMA(())],
        ),
        out_shape=jax.ShapeDtypeStruct(zeros.shape, F32),
        input_output_aliases={2: 0},
        compiler_params=_cparams(("arbitrary",)),
        name="moe_dispatch",
    )(slots, h, zeros)


def _expert_up(xs, wg, wu, blk_exp, n_used):
    S, D = xs.shape
    DE = wg.shape[2]
    nb = S // MOE_TM

    def body(be_ref, nu_ref, x_ref, wg_ref, wu_ref, o_ref):
        @pl.when(pl.program_id(0) < nu_ref[0])
        def _():
            x = x_ref[...].astype(BF16)
            g = jnp.dot(x, wg_ref[...], preferred_element_type=F32)
            u = jnp.dot(x, wu_ref[...], preferred_element_type=F32)
            o_ref[...] = (_silu(g) * u).astype(o_ref.dtype)

    def row(b, be, nu):
        return (jnp.minimum(b, nu[0] - 1), 0)

    return pl.pallas_call(
        body,
        grid_spec=pltpu.PrefetchScalarGridSpec(
            num_scalar_prefetch=2,
            grid=(nb,),
            in_specs=[pl.BlockSpec((MOE_TM, D), row),
                      pl.BlockSpec((None, D, DE), lambda b, be, nu: (be[b], 0, 0)),
                      pl.BlockSpec((None, D, DE), lambda b, be, nu: (be[b], 0, 0))],
            out_specs=pl.BlockSpec((MOE_TM, DE), row),
        ),
        out_shape=jax.ShapeDtypeStruct((S, DE), BF16),
        compiler_params=_cparams(("arbitrary",)),
        name="moe_expert_up",
    )(blk_exp, n_used, xs, wg, wu)


def _expert_down(hid, wd, blk_exp, n_used):
    S, DE = hid.shape
    D = wd.shape[2]
    nb = S // MOE_TM

    def body(be_ref, nu_ref, h_ref, wd_ref, o_ref):
        @pl.when(pl.program_id(0) < nu_ref[0])
        def _():
            o_ref[...] = jnp.dot(h_ref[...], wd_ref[...], preferred_element_type=F32)

    def row(b, be, nu):
        return (jnp.minimum(b, nu[0] - 1), 0)

    return pl.pallas_call(
        body,
        grid_spec=pltpu.PrefetchScalarGridSpec(
            num_scalar_prefetch=2,
            grid=(nb,),
            in_specs=[pl.BlockSpec((MOE_TM, DE), row),
                      pl.BlockSpec((None, DE, D), lambda b, be, nu: (be[b], 0, 0))],
            out_specs=pl.BlockSpec((MOE_TM, D), row),
        ),
        out_shape=jax.ShapeDtypeStruct((S, D), F32),
        compiler_params=_cparams(("arbitrary",)),
        name="moe_expert_down",
    )(blk_exp, n_used, hid, wd)


def _combine(name, out_slots, slots, gates, xres, gate2, final_g):
    M, D = xres.shape
    tb = COMBINE_TB
    nsteps = M // tb

    def body(slots_ref, src_ref, g_ref, x_ref, g2_ref, fg_ref, o_ref, buf, sem):
        i = pl.program_id(0)

        def row_copy(s, b, k, r):
            return pltpu.make_async_copy(src_ref.at[pl.ds(s, 1)], buf.at[b, k, pl.ds(r, 1)], sem.at[b])

        def issue(blk, b):
            def tok(r, c):
                for k in range(TOP_K):
                    row_copy(slots_ref[(blk * tb + r) * TOP_K + k], b, k, r).start()
                return c
            lax.fori_loop(0, tb, tok, 0)

        @pl.when(i == 0)
        def _():
            issue(0, 0)

        @pl.when(i + 1 < nsteps)
        def _():
            issue(i + 1, (i + 1) % 2)

        b = i % 2

        def w(j, c):
            row_copy(0, b, 0, 0).wait()
            return c
        lax.fori_loop(0, tb * TOP_K, w, 0)

        g = g_ref[...]
        routed = g[:, 0:1] * buf[b, 0]
        for k in range(1, TOP_K):
            routed = routed + g[:, k:k + 1] * buf[b, k]
        x2 = x_ref[...] + g2_ref[...] * routed
        y = x2 * lax.rsqrt(jnp.mean(x2 * x2, axis=-1, keepdims=True) + RMS_EPS)
        o_ref[...] = y * fg_ref[...]

    return pl.pallas_call(
        body,
        grid_spec=pltpu.PrefetchScalarGridSpec(
            num_scalar_prefetch=1,
            grid=(nsteps,),
            in_specs=[pl.BlockSpec(memory_space=pl.ANY),
                      pl.BlockSpec((tb, TOP_K), lambda i, s: (i, 0)),
                      pl.BlockSpec((tb, D), lambda i, s: (i, 0)),
                      (pl.BlockSpec((1, D), lambda i, s: (0, 0)) if gate2.shape[0] == 1
                       else pl.BlockSpec((tb, D), lambda i, s: (i, 0))),
                      pl.BlockSpec((1, D), lambda i, s: (0, 0))],
            out_specs=pl.BlockSpec((tb, D), lambda i, s: (i, 0)),
            scratch_shapes=[pltpu.VMEM((2, TOP_K, tb, D), F32), pltpu.SemaphoreType.DMA((2,))],
        ),
        out_shape=jax.ShapeDtypeStruct((M, D), F32),
        compiler_params=_cparams(("arbitrary",)),
        name=name,
    )(slots, out_slots, gates, xres, gate2, final_g)


def _mixer_sublayer(tag, x, mod, w, attend, *, tm, lc):
    M, D = x.shape
    h = _normmod(f"{tag}_norm1", x, w["norm1_g"], mod["scale1"], mod["shift1"], tm=NORM_TM, out_dtype=BF16)
    ident = lambda accs, aux: [accs[0]]
    both = lambda accs, aux: [accs[0], accs[0]]
    (q,) = _mm(f"{tag}_q", [h], [(0, w["w_q"])], [], ident, [BF16], tm=tm, tn=512)
    k32, k16 = _mm(f"{tag}_k", [h], [(0, w["w_k"])], [], both, [F32, BF16], tm=tm, tn=512)
    v32, v16 = _mm(f"{tag}_v", [h], [(0, w["w_v"])], [], both, [F32, BF16], tm=tm, tn=512)
    (logf_pad,) = _mm(f"{tag}_f", [h], [(0, w["w_f"])], [w["b_f"]],
                      lambda accs, aux: [_log_sigmoid(accs[0] + aux[0])], [F32], tm=tm, tn=LANES)
    logf = logf_pad[:, :N_HEADS]
    (u,) = _mm(f"{tag}_u", [h], [(0, w["w_u"])], [], lambda accs, aux: [_gelu(accs[0])], [F32], tm=tm, tn=512)
    (gv,) = _mm(f"{tag}_vb", [h], [(0, w["w_vb"])], [], lambda accs, aux: [_gelu(accs[0])], [F32], tm=tm, tn=512)
    sig = lambda accs, aux: [jax.nn.sigmoid(accs[0])]
    (ga,) = _mm(f"{tag}_ga", [h], [(0, w["w_ga"])], [], sig, [BF16], tm=tm, tn=512)
    (gb,) = _mm(f"{tag}_gb", [h], [(0, w["w_gb"])], [], sig, [BF16], tm=tm, tn=512)

    attn = attend(q, k16, v16, logf)
    vb, spatial = _spatial(f"{tag}_spatial", gv, u, w["gm_ln_g"], w["gm_ln_b"], w["gm_ws"][:, :lc, :lc],
                           w["gm_bs"][:, :lc, None], lc=lc, tm=tm)
    (merged,) = _mm(f"{tag}_merge", [attn, spatial], [(0, w["w_a"]), (1, w["w_b"])], [ga, gb],
                    lambda accs, aux: [aux[0] * accs[0] + aux[1] * accs[1]], [BF16], tm=tm, tn=512)
    (x1,) = _mm(f"{tag}_out", [merged], [(0, w["w_out"])], [x, mod["gate1"]],
                lambda accs, aux: [aux[0] + aux[1] * accs[0]], [F32], tm=tm, tn=512)
    return x1, k32, v32, logf, vb


def _pad_lanes(x):
    pad = (-x.shape[-1]) % LANES
    return jnp.pad(x, [(0, 0)] * (x.ndim - 1) + [(0, pad)]) if pad else x


def kernel(x_prompt, x_sample, cache_k, cache_v, cache_logf, c_prompt, c_sample, ada_w, ada_b, norm1_g, w_in, b_forget, gm_ln_g, gm_ln_b, gm_ws, gm_bs, w_branch_a, w_branch_b, w_out, norm2_g, w_router, router_bias, w_exp_gate, w_exp_up, w_exp_down, w_sh_gate, w_sh_up, w_sh_down, final_g):
    assert ada_w.shape[0] == 1, "single trunk layer"
    B, L, D = x_prompt.shape
    SB, ST, _ = x_sample.shape
    assert B == 1
    P = cache_k.shape[2]
    Mp, Ms = B * L, SB * ST
    M = Mp + Ms

    wi = w_in[0]
    c_q, c_k, c_v, c_f = 0, WIDTH_A, 2 * WIDTH_A, 3 * WIDTH_A
    c_u = c_f + N_HEADS
    c_vb = c_u + GM_WIDTH
    c_ga = c_vb + GM_WIDTH
    c_gb = c_ga + D
    w = {
        "norm1_g": norm1_g, "gm_ln_g": gm_ln_g, "gm_ln_b": gm_ln_b, "gm_ws": gm_ws[0], "gm_bs": gm_bs[0],
        "w_q": wi[:, c_q:c_k].astype(BF16), "w_k": wi[:, c_k:c_v].astype(BF16), "w_v": wi[:, c_v:c_f].astype(BF16),
        "w_f": _pad_lanes(wi[:, c_f:c_u]).astype(BF16), "b_f": _pad_lanes(b_forget),
        "w_u": wi[:, c_u:c_vb].astype(BF16), "w_vb": wi[:, c_vb:c_ga].astype(BF16),
        "w_ga": wi[:, c_ga:c_gb].astype(BF16), "w_gb": wi[:, c_gb:].astype(BF16),
        "w_a": w_branch_a[0].astype(BF16), "w_b": w_branch_b[0].astype(BF16), "w_out": w_out[0].astype(BF16),
    }

    c_all = jnp.concatenate([c_prompt, c_sample], axis=0)
    n_c = c_all.shape[0]
    c_all = jnp.pad(c_all, ((0, (-n_c) % 32), (0, 0)))
    (mod,) = _mm("ada_mod", [c_all], [(0, ada_w[0])], [ada_b], lambda accs, aux: [accs[0] + aux[0]], [F32],
                 tm=c_all.shape[0], tn=512, x_pre=_silu)
    names = ["shift1", "scale1", "gate1", "shift2", "scale2", "gate2"]
    mod_p = {n: mod[0:B, i * D:(i + 1) * D] for i, n in enumerate(names)}
    mod_s = {n: jnp.broadcast_to(mod[B:B + SB, None, i * D:(i + 1) * D], (SB, ST, D)).reshape(Ms, D)
             for i, n in enumerate(names)}

    def attend_prompt(q, k16, v16, logf):
        cum_t = _cumsum_lanes("p_cumsum", jnp.transpose(logf)[None])[0]
        return _fox_prompt(q, k16, v16, cum_t[:, :, None], cum_t[:, None, :], tq=512)

    def attend_sample(q, k16, v16, logf):
        lf = jnp.concatenate([cache_logf[0].astype(F32), logf.reshape(SB, ST, N_HEADS)], axis=1)
        lf_t = _pad_lanes(jnp.transpose(lf, (0, 2, 1)))
        cum_t = _cumsum_lanes("s_cumsum", lf_t)
        ck_cache = cum_t[:, :, None, :P]
        ck_new = cum_t[:, :, None, P:P + ST]
        cq = cum_t[:, :, P:P + ST, None]
        o = _fox_sample(q.reshape(SB, ST, WIDTH_A), k16.reshape(SB, ST, WIDTH_A), v16.reshape(SB, ST, WIDTH_A),
                        cache_k[0].reshape(SB, P, WIDTH_A), cache_v[0].reshape(SB, P, WIDTH_A),
                        cq, ck_cache, ck_new, tk=512)
        return o.reshape(Ms, WIDTH_A)

    x1p, kp, vp, fp, _ = _mixer_sublayer("p", x_prompt.reshape(Mp, D), mod_p, w, attend_prompt, tm=512,
                                         lc=min(L, GM_CHUNK))
    x1s, ks, vs, fs, gs = _mixer_sublayer("s", x_sample.reshape(Ms, D), mod_s, w, attend_sample, tm=512,
                                          lc=min(ST, GM_CHUNK))

    wr = jnp.transpose(w_router[0])
    wr_hi = wr.astype(BF16)
    wr_lo = (wr - wr_hi.astype(F32)).astype(BF16)
    h2p, lgp = _normmod_router("p_norm2", x1p, norm2_g, mod_p["scale2"], mod_p["shift2"], wr_hi, wr_lo, tm=NORM_TM)
    h2s, lgs = _normmod_router("s_norm2", x1s, norm2_g, mod_s["scale2"], mod_s["shift2"], wr_hi, wr_lo, tm=NORM_TM)
    h2 = jnp.concatenate([h2p, h2s], axis=0)
    logits_t = jnp.concatenate([lgp, lgs], axis=1)

    idx_t, gates_t, onehot_t = _route(logits_t, jnp.transpose(router_bias).astype(F32), tt=512)
    pos_t = _cumsum_lanes("moe_rank", onehot_t[None])[0]
    counts = pos_t[:, M - 1].astype(I32)
    padded = (counts + MOE_TM - 1) // MOE_TM * MOE_TM
    pad_end = jnp.cumsum(padded)
    pad_start = pad_end - padded
    n_blocks = -(-(M * TOP_K + N_EXPERTS * (MOE_TM - 1)) // MOE_TM)
    n_slots = n_blocks * MOE_TM
    n_used = (pad_end[N_EXPERTS - 1] // MOE_TM).astype(I32).reshape(1)
    blk_ids = jnp.minimum(jnp.arange(n_blocks, dtype=I32), n_used[0] - 1)
    blk_exp = jnp.minimum(jnp.searchsorted(pad_end, blk_ids * MOE_TM, side="right"), N_EXPERTS - 1).astype(I32)
    slot_t = _slots(idx_t, pos_t, pad_start.astype(F32)[:, None], tt=512)
    slots = jnp.transpose(slot_t).reshape(M * TOP_K)
    gates = jnp.transpose(gates_t)

    xs_sorted = _dispatch(h2, slots, jnp.zeros((n_slots, D), F32))
    hid = _expert_up(xs_sorted, w_exp_gate[0].astype(BF16), w_exp_up[0].astype(BF16), blk_exp, n_used)
    out_slots = _expert_down(hid, w_exp_down[0].astype(BF16), blk_exp, n_used)

    wsg, wsu, wsd = w_sh_gate[0].astype(BF16), w_sh_up[0].astype(BF16), w_sh_down[0].astype(BF16)
    swi = lambda accs, aux: [_silu(accs[0]) * accs[1]]
    resid = lambda accs, aux: [aux[0] + aux[1] * accs[0]]

    def finish(tag, h2g, x1g, mod_g, slots_g, gates_g):
        (hs,) = _mm(f"{tag}_sh_up", [h2g], [(0, wsg), (0, wsu)], [], swi, [BF16], tm=512, tn=512)
        (xres,) = _mm(f"{tag}_sh_down", [hs], [(0, wsd)], [x1g, mod_g["gate2"]], resid, [F32], tm=512, tn=512)
        return _combine(f"{tag}_combine", out_slots, slots_g, gates_g, xres, mod_g["gate2"], final_g[None, :])

    y_p = finish("p", h2p, x1p, mod_p, slots[:Mp * TOP_K], gates[:Mp])
    y_s = finish("s", h2s, x1s, mod_s, slots[Mp * TOP_K:], gates[Mp:])

    hd = (N_HEADS, HEAD_DIM)
    return (y_p.reshape(B, L, D), y_s.reshape(SB, ST, D),
            kp.reshape(1, B, L, *hd), vp.reshape(1, B, L, *hd), fp.reshape(1, B, L, N_HEADS),
            ks.reshape(1, SB, ST, *hd), vs.reshape(1, SB, ST, *hd), fs.reshape(1, SB, ST, N_HEADS),
            gs.reshape(1, SB, ST, GM_WIDTH))
```

```python
import functools
import math

import jax
import jax.numpy as jnp
import numpy as np
from jax import lax
from jax.experimental import pallas as pl
from jax.experimental.pallas import tpu as pltpu

F32 = jnp.float32
BF16 = jnp.bfloat16
I32 = jnp.int32

N_HEADS = 16
HEAD_DIM = 128
WIDTH_A = N_HEADS * HEAD_DIM
GM_GROUPS = 8
GM_WIDTH = 2048
GM_GROUP_DIM = GM_WIDTH // GM_GROUPS
GM_CHUNK = 128
N_EXPERTS = 64
TOP_K = 8
N_EXPERT_GROUPS = 8
TOPK_GROUPS = 4
ROUTED_SCALE = 2.5
RMS_EPS = 1e-6
LN_EPS = 1e-5

LANES = 128
NORM_TM = 256
MOE_TM = 256
COMBINE_TB = 32
VMEM_LIMIT = 56 << 20


def _cparams(sem, vmem=VMEM_LIMIT):
    return pltpu.CompilerParams(dimension_semantics=sem, vmem_limit_bytes=vmem)


def _mm(name, xs, ws, aux, epilogue, out_dtypes, *, tm, tn, x_pre=None):
    M = xs[0].shape[0]
    N = ws[0][1].shape[1]
    assert M % tm == 0 and N % tn == 0, (name, M, N, tm, tn)
    in_specs, args = [], []
    for x in xs:
        in_specs.append(pl.BlockSpec((tm, x.shape[1]), lambda i, j: (i, 0)))
        args.append(x)
    for _, w in ws:
        in_specs.append(pl.BlockSpec((w.shape[0], tn), lambda i, j: (0, j)))
        args.append(w)
    for a in aux:
        if a.shape[0] == 1:
            in_specs.append(pl.BlockSpec((1, tn), lambda i, j: (0, j)))
        else:
            assert a.shape == (M, N), (name, a.shape)
            in_specs.append(pl.BlockSpec((tm, tn), lambda i, j: (i, j)))
        args.append(a)
    n_x, n_w, n_a = len(xs), len(ws), len(aux)
    x_of_w = [xi for xi, _ in ws]

    def body(*refs):
        x_refs = refs[:n_x]
        w_refs = refs[n_x:n_x + n_w]
        a_refs = refs[n_x + n_w:n_x + n_w + n_a]
        o_refs = refs[n_x + n_w + n_a:]
        xv = []
        for r in x_refs:
            x = r[...]
            if x_pre is not None:
                x = x_pre(x)
            xv.append(x.astype(BF16))
        accs = [jnp.dot(xv[xi], w_refs[k][...].astype(BF16), preferred_element_type=F32)
                for k, xi in enumerate(x_of_w)]
        res = epilogue(accs, [r[...].astype(F32) for r in a_refs])
        for o, r in zip(o_refs, res):
            o[...] = r.astype(o.dtype)

    return pl.pallas_call(
        body,
        grid=(M // tm, N // tn),
        in_specs=in_specs,
        out_specs=[pl.BlockSpec((tm, tn), lambda i, j: (i, j)) for _ in out_dtypes],
        out_shape=[jax.ShapeDtypeStruct((M, N), d) for d in out_dtypes],
        compiler_params=_cparams(("parallel", "arbitrary")),
        name=name,
    )(*args)


def _gelu(x):
    return x * (lax.erf(x * (1.0 / math.sqrt(2.0))) + 1.0) * 0.5


def _log_sigmoid(z):
    return jnp.minimum(z, 0.0) - jnp.log1p(jnp.exp(-jnp.abs(z)))


def _silu(x):
    return x * jax.nn.sigmoid(x)


def _aux_spec(a, tm, d):
    if a.shape[0] == 1:
        return pl.BlockSpec((1, d), lambda i: (0, 0))
    return pl.BlockSpec((tm, d), lambda i: (i, 0))


def _normmod_math(x, g, scale, shift):
    y = x * lax.rsqrt(jnp.mean(x * x, axis=-1, keepdims=True) + RMS_EPS)
    return (y * g) * (1.0 + scale) + shift


def _normmod(name, x, g, scale, shift, *, tm, out_dtype):
    M, D = x.shape

    def body(x_ref, g_ref, sc_ref, sh_ref, o_ref):
        o_ref[...] = _normmod_math(x_ref[...], g_ref[...], sc_ref[...], sh_ref[...]).astype(o_ref.dtype)

    return pl.pallas_call(
        body,
        grid=(M // tm,),
        in_specs=[pl.BlockSpec((tm, D), lambda i: (i, 0)), pl.BlockSpec((1, D), lambda i: (0, 0)),
                  _aux_spec(scale, tm, D), _aux_spec(shift, tm, D)],
        out_specs=pl.BlockSpec((tm, D), lambda i: (i, 0)),
        out_shape=jax.ShapeDtypeStruct((M, D), out_dtype),
        compiler_params=_cparams(("parallel",)),
        name=name,
    )(x, g, scale, shift)


def _normmod_router(name, x, g, scale, shift, wr_hi, wr_lo, *, tm):
    M, D = x.shape
    E = wr_hi.shape[0]
    nt = (((1,), (1,)), ((), ()))

    def body(x_ref, g_ref, sc_ref, sh_ref, whi_ref, wlo_ref, h_ref, lg_ref):
        h = _normmod_math(x_ref[...], g_ref[...], sc_ref[...], sh_ref[...])
        h_ref[...] = h
        h_hi = h.astype(BF16)
        h_lo = (h - h_hi.astype(F32)).astype(BF16)
        whi = whi_ref[...]
        lg = lax.dot_general(whi, h_hi, nt, preferred_element_type=F32)
        lg = lg + lax.dot_general(wlo_ref[...], h_hi, nt, preferred_element_type=F32)
        lg = lg + lax.dot_general(whi, h_lo, nt, preferred_element_type=F32)
        lg_ref[...] = lg

    return pl.pallas_call(
        body,
        grid=(M // tm,),
        in_specs=[pl.BlockSpec((tm, D), lambda i: (i, 0)), pl.BlockSpec((1, D), lambda i: (0, 0)),
                  _aux_spec(scale, tm, D), _aux_spec(shift, tm, D),
                  pl.BlockSpec((E, D), lambda i: (0, 0)), pl.BlockSpec((E, D), lambda i: (0, 0))],
        out_specs=[pl.BlockSpec((tm, D), lambda i: (i, 0)), pl.BlockSpec((E, tm), lambda i: (0, i))],
        out_shape=[jax.ShapeDtypeStruct((M, D), F32), jax.ShapeDtypeStruct((E, M), F32)],
        compiler_params=_cparams(("parallel",)),
        name=name,
    )(x, g, scale, shift, wr_hi, wr_lo)


def _scan_lanes(x):
    lane = lax.broadcasted_iota(I32, x.shape, 1)
    d = 1
    while d < LANES:
        x = x + jnp.where(lane >= d, pltpu.roll(x, d, axis=1), 0.0)
        d *= 2
    return x


def _cumsum_lanes(name, x):
    B, R, L = x.shape
    assert L % LANES == 0
    n_chunks = L // LANES

    def body(x_ref, o_ref):
        def step(c, carry):
            off = pl.multiple_of(c * LANES, LANES)
            s = _scan_lanes(x_ref[:, pl.ds(off, LANES)]) + carry
            o_ref[:, pl.ds(off, LANES)] = s
            return s[:, LANES - 1:LANES]
        lax.fori_loop(0, n_chunks, step, jnp.zeros((R, 1), F32))

    return pl.pallas_call(
        body,
        grid=(B,),
        in_specs=[pl.BlockSpec((None, R, L), lambda b: (b, 0, 0))],
        out_specs=pl.BlockSpec((None, R, L), lambda b: (b, 0, 0)),
        out_shape=jax.ShapeDtypeStruct((B, R, L), F32),
        compiler_params=_cparams(("parallel",)),
        name=name,
    )(x)


_NT = (((1,), (1,)), ((), ()))
_SM_SCALE = HEAD_DIM ** -0.5
_EXP2_SCALE = _SM_SCALE * math.log2(math.e)


def _online_update(s, v, m_prev, l_prev, acc_prev):
    m_new = jnp.maximum(m_prev, jnp.max(s, axis=-1, keepdims=True))
    alpha = jnp.exp(m_prev - m_new)
    p = jnp.exp(s - m_new)
    l_new = alpha * l_prev + jnp.sum(p, axis=-1, keepdims=True)
    acc_new = alpha * acc_prev + jnp.dot(p.astype(BF16), v, preferred_element_type=F32)
    return m_new, l_new, acc_new


def _causal(s):
    row = lax.broadcasted_iota(I32, s.shape, 0)
    col = lax.broadcasted_iota(I32, s.shape, 1)
    return jnp.where(col <= row, s, -jnp.inf)


def _fox_augment(q, k, cum_col, *, tr):
    L = q.shape[0]
    H = cum_col.shape[0]

    def body(q_ref, k_ref, c_ref, qa_ref, ka_ref):
        c = c_ref[...] * (1.0 / _SM_SCALE)
        c1 = c.astype(BF16).astype(F32)
        r = c - c1
        c2 = r.astype(BF16).astype(F32)
        c3 = (r - c2).astype(BF16).astype(F32)
        lane = lax.broadcasted_iota(I32, (tr, LANES), 1)
        eq = jnp.where(lane == 0, c1, jnp.where(lane == 1, c2, jnp.where(lane == 2, c3,
                                                                      jnp.where(lane < 6, 1.0, 0.0))))
        ek = jnp.where(lane < 3, 1.0, jnp.where(lane == 3, -c1, jnp.where(lane == 4, -c2,
                                                                        jnp.where(lane == 5, -c3, 0.0))))
        qa_ref[:, :HEAD_DIM] = q_ref[...]
        qa_ref[:, HEAD_DIM:] = eq.astype(BF16)
        ka_ref[:, :HEAD_DIM] = k_ref[...]
        ka_ref[:, HEAD_DIM:] = ek.astype(BF16)

    return pl.pallas_call(
        body,
        grid=(L // tr, H),
        in_specs=[pl.BlockSpec((tr, HEAD_DIM), lambda i, h: (i, h)),
                  pl.BlockSpec((tr, HEAD_DIM), lambda i, h: (i, h)),
                  pl.BlockSpec((None, tr, 1), lambda i, h: (h, i, 0))],
        out_specs=[pl.BlockSpec((tr, 2 * HEAD_DIM), lambda i, h: (i, h)),
                   pl.BlockSpec((tr, 2 * HEAD_DIM), lambda i, h: (i, h))],
        out_shape=[jax.ShapeDtypeStruct((L, H * 2 * HEAD_DIM), BF16)] * 2,
        compiler_params=_cparams(("parallel", "parallel")),
        name="fox_augment",
    )(q, k, cum_col)


def _fox_prompt2(qa, ka, v, *, tb, half, rc):
    L = v.shape[0]
    H = v.shape[1] // HEAD_DIM
    nq = L // tb
    pairs = [(i, j) for i in range(nq) for j in range(i + 1)]
    qi_tab = jnp.asarray(np.array([p[0] for p in pairs], np.int32))
    ki_tab = jnp.asarray(np.array([p[1] for p in pairs], np.int32))
    n_half = tb // half

    def body(qi_ref, ki_ref, qa_ref, ka_ref, v_ref, o_ref, m_sc, l_sc, al_sc, acc_sc, s_sc, p_sc):
        t = pl.program_id(1)
        qb = qi_ref[t]
        kb = ki_ref[t]

        @pl.when(kb == 0)
        def _():
            m_sc[...] = jnp.full(m_sc.shape, -jnp.inf, F32)
            l_sc[...] = jnp.zeros(l_sc.shape, F32)
            acc_sc[...] = jnp.zeros(acc_sc.shape, F32)

        def softmax_rows(r0, s_c):
            rows = slice(r0, r0 + rc)
            nc = s_c.shape[1]
            m_prev = m_sc[rows, :]
            m_next = jnp.maximum(m_prev, jnp.max(s_c, axis=1, keepdims=True))
            m_rep = jnp.concatenate([m_next] * (nc // LANES), axis=1) if nc > LANES else m_next
            p = jnp.exp2((s_c - m_rep) * _EXP2_SCALE)
            alpha = jnp.exp2((m_prev - m_next) * _EXP2_SCALE)
            l_sc[rows, :] = alpha * l_sc[rows, :] + jnp.sum(p, axis=1, keepdims=True)
            m_sc[rows, :] = m_next
            al_sc[rows, :] = alpha
            p_sc[rows, :nc] = p.astype(BF16)

        def half_block(hf, ncols, diag):
            lo = hf * half
            hrows = slice(lo, lo + half)
            s_sc[hrows, :ncols] = lax.dot_general(qa_ref[hrows, :], ka_ref[:ncols, :], _NT,
                                                  preferred_element_type=F32)
            for ci in range(half // rc):
                r0 = lo + ci * rc
                rows = slice(r0, r0 + rc)
                if diag:
                    jt = r0 // LANES
                    nc = (jt + 1) * LANES
                    tile = s_sc[rows, jt * LANES:nc]
                    rowi = lax.broadcasted_iota(I32, (rc, LANES), 0) + (r0 % LANES)
                    coli = lax.broadcasted_iota(I32, (rc, LANES), 1)
                    tile = jnp.where(coli <= rowi, tile, -jnp.inf)
                    s_c = jnp.concatenate([s_sc[rows, :jt * LANES], tile], axis=1) if jt > 0 else tile
                    softmax_rows(r0, s_c)
                    if nc < ncols:
                        p_sc[rows, nc:ncols] = jnp.zeros((rc, ncols - nc), BF16)
                else:
                    softmax_rows(r0, s_sc[rows, :ncols])
            pv = jnp.dot(p_sc[hrows, :ncols], v_ref[:ncols, :], preferred_element_type=F32)
            acc_sc[hrows, :] = al_sc[hrows, :] * acc_sc[hrows, :] + pv

        @pl.when(kb < qb)
        def _():
            for hf in range(n_half):
                half_block(hf, tb, False)

        @pl.when(kb == qb)
        def _():
            for hf in range(n_half):
                half_block(hf, (hf + 1) * half, True)
            o_ref[...] = (acc_sc[...] / l_sc[...]).astype(o_ref.dtype)

    return pl.pallas_call(
        body,
        grid_spec=pltpu.PrefetchScalarGridSpec(
            num_scalar_prefetch=2,
            grid=(H, len(pairs)),
            in_specs=[
                pl.BlockSpec((tb, 2 * HEAD_DIM), lambda h, t, qi, ki: (qi[t], h)),
                pl.BlockSpec((tb, 2 * HEAD_DIM), lambda h, t, qi, ki: (ki[t], h)),
                pl.BlockSpec((tb, HEAD_DIM), lambda h, t, qi, ki: (ki[t], h)),
            ],
            out_specs=pl.BlockSpec((tb, HEAD_DIM), lambda h, t, qi, ki: (qi[t], h)),
            scratch_shapes=[pltpu.VMEM((tb, LANES), F32), pltpu.VMEM((tb, LANES), F32), pltpu.VMEM((tb, LANES), F32),
                            pltpu.VMEM((tb, HEAD_DIM), F32), pltpu.VMEM((tb, tb), F32), pltpu.VMEM((tb, tb), BF16)],
        ),
        out_shape=jax.ShapeDtypeStruct((L, H * HEAD_DIM), BF16),
        compiler_params=_cparams(("parallel", "arbitrary")),
        name="fox_prompt",
    )(qi_tab, ki_tab, qa, ka, v)


def _fox_sample(q, k_new, v_new, cache_k, cache_v, cq, ck_cache, ck_new, *, tk):
    B, T, W = q.shape
    P = cache_k.shape[1]
    nk = P // tk

    def body(q_ref, kn_ref, vn_ref, ck_ref, cv_ref, cq_ref, cc_ref, cn_ref, o_ref, m_sc, l_sc, acc_sc):
        c = pl.program_id(1)

        @pl.when(c == 0)
        def _():
            for h in range(N_HEADS):
                sl = slice(h * HEAD_DIM, (h + 1) * HEAD_DIM)
                s = lax.dot_general(q_ref[:, sl], kn_ref[:, sl], _NT, preferred_element_type=F32)
                s = _causal(s * _SM_SCALE + (cq_ref[h] - cn_ref[h]))
                m, l, acc = _online_update(s, vn_ref[:, sl], jnp.full((T, 1), -jnp.inf, F32),
                                           jnp.zeros((T, 1), F32), jnp.zeros((T, HEAD_DIM), F32))
                m_sc[h] = m
                l_sc[h] = l
                acc_sc[h] = acc

        for h in range(N_HEADS):
            sl = slice(h * HEAD_DIM, (h + 1) * HEAD_DIM)
            kh = ck_ref[:, h, :].astype(BF16)
            vh = cv_ref[:, h, :].astype(BF16)
            s = lax.dot_general(q_ref[:, sl], kh, _NT, preferred_element_type=F32)
            s = s * _SM_SCALE + (cq_ref[h] - cc_ref[h])
            m, l, acc = _online_update(s, vh, m_sc[h], l_sc[h], acc_sc[h])
            m_sc[h] = m
            l_sc[h] = l
            acc_sc[h] = acc

        @pl.when(c == nk - 1)
        def _():
            for h in range(N_HEADS):
                sl = slice(h * HEAD_DIM, (h + 1) * HEAD_DIM)
                o_ref[:, sl] = (acc_sc[h] / l_sc[h]).astype(o_ref.dtype)

    return pl.pallas_call(
        body,
        grid=(B, nk),
        in_specs=[
            pl.BlockSpec((None, T, W), lambda b, c: (b, 0, 0)),
            pl.BlockSpec((None, T, W), lambda b, c: (b, 0, 0)),
            pl.BlockSpec((None, T, W), lambda b, c: (b, 0, 0)),
            pl.BlockSpec((None, tk, N_HEADS, HEAD_DIM), lambda b, c: (b, c, 0, 0)),
            pl.BlockSpec((None, tk, N_HEADS, HEAD_DIM), lambda b, c: (b, c, 0, 0)),
            pl.BlockSpec((None, N_HEADS, T, 1), lambda b, c: (b, 0, 0, 0)),
            pl.BlockSpec((None, N_HEADS, 1, tk), lambda b, c: (b, 0, 0, c)),
            pl.BlockSpec((None, N_HEADS, 1, T), lambda b, c: (b, 0, 0, 0)),
        ],
        out_specs=pl.BlockSpec((None, T, W), lambda b, c: (b, 0, 0)),
        out_shape=jax.ShapeDtypeStruct((B, T, W), BF16),
        scratch_shapes=[pltpu.VMEM((N_HEADS, T, 1), F32), pltpu.VMEM((N_HEADS, T, 1), F32),
                        pltpu.VMEM((N_HEADS, T, HEAD_DIM), F32)],
        compiler_params=_cparams(("parallel", "arbitrary")),
        name="fox_sample",
    )(q, k_new, v_new, cache_k, cache_v, cq, ck_cache, ck_new)


def _spatial(name, gv, u, ln_g, ln_b, ws, bs, *, lc, tm):
    M = gv.shape[0]
    n_chunks = tm // lc

    def body(gv_ref, u_ref, lg_ref, lb_ref, ws_ref, bs_ref, vb_ref, sp_ref):
        x = gv_ref[...]
        mu = jnp.mean(x, axis=-1, keepdims=True)
        xc = x - mu
        var = jnp.mean(xc * xc, axis=-1, keepdims=True)
        vb = xc * lax.rsqrt(var + LN_EPS) * lg_ref[...] + lb_ref[...]
        vb_ref[...] = vb
        row = lax.broadcasted_iota(I32, (lc, lc), 0)
        col = lax.broadcasted_iota(I32, (lc, lc), 1)
        for g in range(GM_GROUPS):
            w = jnp.where(col <= row, ws_ref[g], 0.0).astype(BF16)
            b = bs_ref[g]
            cs = slice(g * GM_GROUP_DIM, (g + 1) * GM_GROUP_DIM)
            for c in range(n_chunks):
                rs = slice(c * lc, (c + 1) * lc)
                mixed = jnp.dot(w, vb_ref[rs, cs].astype(BF16), preferred_element_type=F32) + b
                sp_ref[rs, cs] = (u_ref[rs, cs] * mixed).astype(sp_ref.dtype)

    return pl.pallas_call(
        body,
        grid=(M // tm,),
        in_specs=[pl.BlockSpec((tm, GM_WIDTH), lambda i: (i, 0)), pl.BlockSpec((tm, GM_WIDTH), lambda i: (i, 0)),
                  pl.BlockSpec((1, GM_WIDTH), lambda i: (0, 0)), pl.BlockSpec((1, GM_WIDTH), lambda i: (0, 0)),
                  pl.BlockSpec((GM_GROUPS, lc, lc), lambda i: (0, 0, 0)),
                  pl.BlockSpec((GM_GROUPS, lc, 1), lambda i: (0, 0, 0))],
        out_specs=[pl.BlockSpec((tm, GM_WIDTH), lambda i: (i, 0)), pl.BlockSpec((tm, GM_WIDTH), lambda i: (i, 0))],
        out_shape=[jax.ShapeDtypeStruct((M, GM_WIDTH), F32), jax.ShapeDtypeStruct((M, GM_WIDTH), BF16)],
        compiler_params=_cparams(("parallel",)),
        name=name,
    )(gv, u, ln_g, ln_b, ws, bs)


def _route(logits_t, bias, *, tt):
    E, M = logits_t.shape
    per_group = E // N_EXPERT_GROUPS
    neg = -jnp.inf

    def body(lg_ref, b_ref, idx_ref, gate_ref, oh_ref):
        scores = jax.nn.sigmoid(lg_ref[...])
        sel = scores + b_ref[...]
        sel3 = sel.reshape(N_EXPERT_GROUPS, per_group, tt)
        io_in = lax.broadcasted_iota(I32, sel3.shape, 1)
        m1 = jnp.max(sel3, axis=1, keepdims=True)
        first = jnp.min(jnp.where(sel3 == m1, io_in, per_group), axis=1, keepdims=True)
        m2 = jnp.max(jnp.where(io_in == first, neg, sel3), axis=1, keepdims=True)
        grp_score = (m1 + m2).reshape(N_EXPERT_GROUPS, tt)
        io_g = lax.broadcasted_iota(I32, grp_score.shape, 0)
        grp_sel = jnp.zeros(grp_score.shape, F32)
        work = grp_score
        for _ in range(TOPK_GROUPS):
            mx = jnp.max(work, axis=0, keepdims=True)
            fi = jnp.min(jnp.where(work == mx, io_g, N_EXPERT_GROUPS), axis=0, keepdims=True)
            pick = io_g == fi
            grp_sel = jnp.where(pick, 1.0, grp_sel)
            work = jnp.where(pick, neg, work)
        keep3 = jnp.broadcast_to(grp_sel.reshape(N_EXPERT_GROUPS, 1, tt), sel3.shape)
        work = jnp.where(keep3 > 0.5, sel3, neg).reshape(E, tt)
        io_e = lax.broadcasted_iota(I32, (E, tt), 0)
        onehot = jnp.zeros((E, tt), F32)
        wts = []
        for k in range(TOP_K):
            mx = jnp.max(work, axis=0, keepdims=True)
            fi = jnp.min(jnp.where(work == mx, io_e, E), axis=0, keepdims=True)
            pick = io_e == fi
            idx_ref[k:k + 1, :] = fi
            wts.append(jnp.sum(jnp.where(pick, scores, 0.0), axis=0, keepdims=True))
            onehot = jnp.where(pick, 1.0, onehot)
            work = jnp.where(pick, neg, work)
        total = wts[0]
        for k in range(1, TOP_K):
            total = total + wts[k]
        for k in range(TOP_K):
            gate_ref[k:k + 1, :] = wts[k] / total * ROUTED_SCALE
        oh_ref[...] = onehot

    return pl.pallas_call(
        body,
        grid=(M // tt,),
        in_specs=[pl.BlockSpec((E, tt), lambda i: (0, i)), pl.BlockSpec((E, 1), lambda i: (0, 0))],
        out_specs=[pl.BlockSpec((TOP_K, tt), lambda i: (0, i)), pl.BlockSpec((TOP_K, tt), lambda i: (0, i)),
                   pl.BlockSpec((E, tt), lambda i: (0, i))],
        out_shape=[jax.ShapeDtypeStruct((TOP_K, M), I32), jax.ShapeDtypeStruct((TOP_K, M), F32),
                   jax.ShapeDtypeStruct((E, M), F32)],
        compiler_params=_cparams(("parallel",)),
        name="moe_route",
    )(logits_t, bias)


def _slots(idx_t, pos_t, start, *, tt):
    K, M = idx_t.shape
    E = pos_t.shape[0]

    def body(idx_ref, pos_ref, st_ref, o_ref):
        base = pos_ref[...] - 1.0 + st_ref[...]
        io_e = lax.broadcasted_iota(I32, (E, tt), 0)
        for k in range(K):
            pick = io_e == idx_ref[k:k + 1, :]
            o_ref[k:k + 1, :] = jnp.sum(jnp.where(pick, base, 0.0), axis=0, keepdims=True).astype(I32)

    return pl.pallas_call(
        body,
        grid=(M // tt,),
        in_specs=[pl.BlockSpec((K, tt), lambda i: (0, i)), pl.BlockSpec((E, tt), lambda i: (0, i)),
                  pl.BlockSpec((E, 1), lambda i: (0, 0))],
        out_specs=pl.BlockSpec((K, tt), lambda i: (0, i)),
        out_shape=jax.ShapeDtypeStruct((K, M), I32),
        compiler_params=_cparams(("parallel",)),
        name="moe_slots",
    )(idx_t, pos_t, start)


def _expert_up(h, slot_tok, wg, wu, blk_exp, n_used):
    M, D = h.shape
    S = slot_tok.shape[0]
    DE = wg.shape[2]
    nb = S // MOE_TM
    unroll = 8

    def body(be_ref, nu_ref, st_ref, h_ref, wg_ref, wu_ref, o_ref, xbuf, sem):
        b = pl.program_id(0)
        nu = nu_ref[0]

        def row_copy(tok, buf, r):
            return pltpu.make_async_copy(h_ref.at[pl.ds(tok, 1)], xbuf.at[buf, pl.ds(r, 1)], sem.at[buf])

        def gather(blk, buf):
            def rows(i, c):
                for j in range(unroll):
                    r = i * unroll + j
                    row_copy(st_ref[blk * MOE_TM + r], buf, r).start()
                return c
            lax.fori_loop(0, MOE_TM // unroll, rows, 0)

        @pl.when(b == 0)
        def _():
            gather(0, 0)

        @pl.when(b + 1 < nu)
        def _():
            gather(b + 1, (b + 1) % 2)

        @pl.when(b < nu)
        def _():
            buf = b % 2

            def waits(i, c):
                for _ in range(unroll):
                    row_copy(0, buf, 0).wait()
                return c
            lax.fori_loop(0, MOE_TM // unroll, waits, 0)
            x = xbuf[buf].astype(BF16)
            g = jnp.dot(x, wg_ref[...], preferred_element_type=F32)
            u = jnp.dot(x, wu_ref[...], preferred_element_type=F32)
            o_ref[...] = (_silu(g) * u).astype(o_ref.dtype)

        @pl.when(b >= nu)
        def _():
            o_ref[...] = jnp.zeros(o_ref.shape, o_ref.dtype)

    return pl.pallas_call(
        body,
        grid_spec=pltpu.PrefetchScalarGridSpec(
            num_scalar_prefetch=3,
            grid=(nb,),
            in_specs=[pl.BlockSpec(memory_space=pl.ANY),
                      pl.BlockSpec((None, D, DE), lambda b, be, nu, st: (be[b], 0, 0)),
                      pl.BlockSpec((None, D, DE), lambda b, be, nu, st: (be[b], 0, 0))],
            out_specs=pl.BlockSpec((MOE_TM, DE), lambda b, be, nu, st: (b, 0)),
            scratch_shapes=[pltpu.VMEM((2, MOE_TM, D), F32), pltpu.SemaphoreType.DMA((2,))],
        ),
        out_shape=jax.ShapeDtypeStruct((S, DE), BF16),
        compiler_params=_cparams(("arbitrary",)),
        name="moe_expert_up",
    )(blk_exp, n_used, slot_tok, h, wg, wu)


def _expert_down(hid, wd, blk_exp, n_used):
    S, DE = hid.shape
    D = wd.shape[2]
    nb = S // MOE_TM

    def body(be_ref, nu_ref, h_ref, wd_ref, o_ref):
        @pl.when(pl.program_id(0) < nu_ref[0])
        def _():
            o_ref[...] = jnp.dot(h_ref[...], wd_ref[...], preferred_element_type=F32)

        @pl.when(pl.program_id(0) >= nu_ref[0])
        def _():
            o_ref[...] = jnp.zeros(o_ref.shape, o_ref.dtype)

    def row(b, be, nu):
        return (jnp.minimum(b, nu[0] - 1), 0)

    return pl.pallas_call(
        body,
        grid_spec=pltpu.PrefetchScalarGridSpec(
            num_scalar_prefetch=2,
            grid=(nb,),
            in_specs=[pl.BlockSpec((MOE_TM, DE), row),
                      pl.BlockSpec((None, DE, D), lambda b, be, nu: (be[b], 0, 0))],
            out_specs=pl.BlockSpec((MOE_TM, D), lambda b, be, nu: (b, 0)),
        ),
        out_shape=jax.ShapeDtypeStruct((S, D), F32),
        compiler_params=_cparams(("arbitrary",)),
        name="moe_expert_down",
    )(blk_exp, n_used, hid, wd)


def _combine(name, out_slots, slots, gates, xres, gate2, final_g):
    M, D = xres.shape
    tb = COMBINE_TB
    nsteps = M // tb

    def body(slots_ref, src_ref, g_ref, x_ref, g2_ref, fg_ref, o_ref, buf, sem):
        i = pl.program_id(0)

        def row_copy(s, b, k, r):
            return pltpu.make_async_copy(src_ref.at[pl.ds(s, 1)], buf.at[b, k, pl.ds(r, 1)], sem.at[b])

        def issue(blk, b):
            def tok(r, c):
                for k in range(TOP_K):
                    row_copy(slots_ref[(blk * tb + r) * TOP_K + k], b, k, r).start()
                return c
            lax.fori_loop(0, tb, tok, 0)

        @pl.when(i == 0)
        def _():
            issue(0, 0)

        @pl.when(i + 1 < nsteps)
        def _():
            issue(i + 1, (i + 1) % 2)

        b = i % 2

        def w(j, c):
            row_copy(0, b, 0, 0).wait()
            return c
        lax.fori_loop(0, tb * TOP_K, w, 0)

        g = g_ref[...]
        routed = g[:, 0:1] * buf[b, 0]
        for k in range(1, TOP_K):
            routed = routed + g[:, k:k + 1] * buf[b, k]
        x2 = x_ref[...] + g2_ref[...] * routed
        y = x2 * lax.rsqrt(jnp.mean(x2 * x2, axis=-1, keepdims=True) + RMS_EPS)
        o_ref[...] = y * fg_ref[...]

    return pl.pallas_call(
        body,
        grid_spec=pltpu.PrefetchScalarGridSpec(
            num_scalar_prefetch=1,
            grid=(nsteps,),
            in_specs=[pl.BlockSpec(memory_space=pl.ANY),
                      pl.BlockSpec((tb, TOP_K), lambda i, s: (i, 0)),
                      pl.BlockSpec((tb, D), lambda i, s: (i, 0)),
                      (pl.BlockSpec((1, D), lambda i, s: (0, 0)) if gate2.shape[0] == 1
                       else pl.BlockSpec((tb, D), lambda i, s: (i, 0))),
                      pl.BlockSpec((1, D), lambda i, s: (0, 0))],
            out_specs=pl.BlockSpec((tb, D), lambda i, s: (i, 0)),
            scratch_shapes=[pltpu.VMEM((2, TOP_K, tb, D), F32), pltpu.SemaphoreType.DMA((2,))],
        ),
        out_shape=jax.ShapeDtypeStruct((M, D), F32),
        compiler_params=_cparams(("arbitrary",)),
        name=name,
    )(slots, out_slots, gates, xres, gate2, final_g)


def _mixer_sublayer(tag, x, mod, w, attend, *, tm, lc):
    M, D = x.shape
    h = _normmod(f"{tag}_norm1", x, w["norm1_g"], mod["scale1"], mod["shift1"], tm=NORM_TM, out_dtype=BF16)
    ident = lambda accs, aux: [accs[0]]
    both = lambda accs, aux: [accs[0], accs[0]]
    (q,) = _mm(f"{tag}_q", [h], [(0, w["w_q"])], [], ident, [BF16], tm=tm, tn=512)
    k32, k16 = _mm(f"{tag}_k", [h], [(0, w["w_k"])], [], both, [F32, BF16], tm=tm, tn=512)
    v32, v16 = _mm(f"{tag}_v", [h], [(0, w["w_v"])], [], both, [F32, BF16], tm=tm, tn=512)
    (logf_pad,) = _mm(f"{tag}_f", [h], [(0, w["w_f"])], [w["b_f"]],
                      lambda accs, aux: [_log_sigmoid(accs[0] + aux[0])], [F32], tm=tm, tn=LANES)
    logf = logf_pad[:, :N_HEADS]
    (u,) = _mm(f"{tag}_u", [h], [(0, w["w_u"])], [], lambda accs, aux: [_gelu(accs[0])], [F32], tm=tm, tn=512)
    (gv,) = _mm(f"{tag}_vb", [h], [(0, w["w_vb"])], [], lambda accs, aux: [_gelu(accs[0])], [F32], tm=tm, tn=512)
    sig = lambda accs, aux: [jax.nn.sigmoid(accs[0])]
    (ga,) = _mm(f"{tag}_ga", [h], [(0, w["w_ga"])], [], sig, [BF16], tm=tm, tn=512)
    (gb,) = _mm(f"{tag}_gb", [h], [(0, w["w_gb"])], [], sig, [BF16], tm=tm, tn=512)

    attn = attend(q, k16, v16, logf)
    vb, spatial = _spatial(f"{tag}_spatial", gv, u, w["gm_ln_g"], w["gm_ln_b"], w["gm_ws"][:, :lc, :lc],
                           w["gm_bs"][:, :lc, None], lc=lc, tm=tm)
    (merged,) = _mm(f"{tag}_merge", [attn, spatial], [(0, w["w_a"]), (1, w["w_b"])], [ga, gb],
                    lambda accs, aux: [aux[0] * accs[0] + aux[1] * accs[1]], [BF16], tm=tm, tn=512)
    (x1,) = _mm(f"{tag}_out", [merged], [(0, w["w_out"])], [x, mod["gate1"]],
                lambda accs, aux: [aux[0] + aux[1] * accs[0]], [F32], tm=tm, tn=512)
    return x1, k32, v32, logf, vb


def _pad_lanes(x):
    pad = (-x.shape[-1]) % LANES
    return jnp.pad(x, [(0, 0)] * (x.ndim - 1) + [(0, pad)]) if pad else x


def kernel(x_prompt, x_sample, cache_k, cache_v, cache_logf, c_prompt, c_sample, ada_w, ada_b, norm1_g, w_in, b_forget, gm_ln_g, gm_ln_b, gm_ws, gm_bs, w_branch_a, w_branch_b, w_out, norm2_g, w_router, router_bias, w_exp_gate, w_exp_up, w_exp_down, w_sh_gate, w_sh_up, w_sh_down, final_g):
    assert ada_w.shape[0] == 1, "single trunk layer"
    B, L, D = x_prompt.shape
    SB, ST, _ = x_sample.shape
    assert B == 1
    P = cache_k.shape[2]
    Mp, Ms = B * L, SB * ST
    M = Mp + Ms

    wi = w_in[0]
    c_q, c_k, c_v, c_f = 0, WIDTH_A, 2 * WIDTH_A, 3 * WIDTH_A
    c_u = c_f + N_HEADS
    c_vb = c_u + GM_WIDTH
    c_ga = c_vb + GM_WIDTH
    c_gb = c_ga + D
    w = {
        "norm1_g": norm1_g, "gm_ln_g": gm_ln_g, "gm_ln_b": gm_ln_b, "gm_ws": gm_ws[0], "gm_bs": gm_bs[0],
        "w_q": wi[:, c_q:c_k].astype(BF16), "w_k": wi[:, c_k:c_v].astype(BF16), "w_v": wi[:, c_v:c_f].astype(BF16),
        "w_f": _pad_lanes(wi[:, c_f:c_u]).astype(BF16), "b_f": _pad_lanes(b_forget),
        "w_u": wi[:, c_u:c_vb].astype(BF16), "w_vb": wi[:, c_vb:c_ga].astype(BF16),
        "w_ga": wi[:, c_ga:c_gb].astype(BF16), "w_gb": wi[:, c_gb:].astype(BF16),
        "w_a": w_branch_a[0].astype(BF16), "w_b": w_branch_b[0].astype(BF16), "w_out": w_out[0].astype(BF16),
    }

    c_all = jnp.concatenate([c_prompt, c_sample], axis=0)
    n_c = c_all.shape[0]
    c_all = jnp.pad(c_all, ((0, (-n_c) % 32), (0, 0)))
    (mod,) = _mm("ada_mod", [c_all], [(0, ada_w[0])], [ada_b], lambda accs, aux: [accs[0] + aux[0]], [F32],
                 tm=c_all.shape[0], tn=512, x_pre=_silu)
    names = ["shift1", "scale1", "gate1", "shift2", "scale2", "gate2"]
    mod_p = {n: mod[0:B, i * D:(i + 1) * D] for i, n in enumerate(names)}
    mod_s = {n: jnp.broadcast_to(mod[B:B + SB, None, i * D:(i + 1) * D], (SB, ST, D)).reshape(Ms, D)
             for i, n in enumerate(names)}

    def attend_prompt(q, k16, v16, logf):
        cum_t = _cumsum_lanes("p_cumsum", jnp.transpose(logf)[None])[0]
        qa, ka = _fox_augment(q, k16, cum_t[:, :, None], tr=512)
        return _fox_prompt2(qa, ka, v16, tb=1024, half=512, rc=32)

    def attend_sample(q, k16, v16, logf):
        lf = jnp.concatenate([cache_logf[0].astype(F32), logf.reshape(SB, ST, N_HEADS)], axis=1)
        lf_t = _pad_lanes(jnp.transpose(lf, (0, 2, 1)))
        cum_t = _cumsum_lanes("s_cumsum", lf_t.reshape(1, SB * N_HEADS, -1)).reshape(lf_t.shape)
        ck_cache = cum_t[:, :, None, :P]
        ck_new = cum_t[:, :, None, P:P + ST]
        cq = cum_t[:, :, P:P + ST, None]
        o = _fox_sample(q.reshape(SB, ST, WIDTH_A), k16.reshape(SB, ST, WIDTH_A), v16.reshape(SB, ST, WIDTH_A),
                        cache_k[0], cache_v[0], cq, ck_cache, ck_new, tk=512)
        return o.reshape(Ms, WIDTH_A)

    x1p, kp, vp, fp, _ = _mixer_sublayer("p", x_prompt.reshape(Mp, D), mod_p, w, attend_prompt, tm=512,
                                         lc=min(L, GM_CHUNK))
    x1s, ks, vs, fs, gs = _mixer_sublayer("s", x_sample.reshape(Ms, D), mod_s, w, attend_sample, tm=512,
                                          lc=min(ST, GM_CHUNK))

    wr = jnp.transpose(w_router[0])
    wr_hi = wr.astype(BF16)
    wr_lo = (wr - wr_hi.astype(F32)).astype(BF16)
    h2p, lgp = _normmod_router("p_norm2", x1p, norm2_g, mod_p["scale2"], mod_p["shift2"], wr_hi, wr_lo, tm=NORM_TM)
    h2s, lgs = _normmod_router("s_norm2", x1s, norm2_g, mod_s["scale2"], mod_s["shift2"], wr_hi, wr_lo, tm=NORM_TM)
    h2 = jnp.concatenate([h2p, h2s], axis=0)
    logits_t = jnp.concatenate([lgp, lgs], axis=1)

    idx_t, gates_t, onehot_t = _route(logits_t, jnp.transpose(router_bias).astype(F32), tt=512)
    pos_t = _cumsum_lanes("moe_rank", onehot_t[None])[0]
    counts = pos_t[:, M - 1].astype(I32)
    padded = (counts + MOE_TM - 1) // MOE_TM * MOE_TM
    pad_end = jnp.cumsum(padded)
    pad_start = pad_end - padded
    n_blocks = -(-(M * TOP_K + N_EXPERTS * (MOE_TM - 1)) // MOE_TM)
    n_slots = n_blocks * MOE_TM
    n_used = (pad_end[N_EXPERTS - 1] // MOE_TM).astype(I32).reshape(1)
    blk_ids = jnp.minimum(jnp.arange(n_blocks, dtype=I32), n_used[0] - 1)
    blk_exp = jnp.sum((pad_end[None, :] <= (blk_ids * MOE_TM)[:, None]).astype(I32), axis=1)
    blk_exp = jnp.minimum(blk_exp, N_EXPERTS - 1)
    slot_t = _slots(idx_t, pos_t, pad_start.astype(F32)[:, None], tt=512)
    slots = jnp.transpose(slot_t).reshape(M * TOP_K)
    gates = jnp.transpose(gates_t)
    slot_tok = jnp.zeros((n_slots,), I32).at[slots].set(jnp.arange(M * TOP_K, dtype=I32) // TOP_K)

    hid = _expert_up(h2, slot_tok, w_exp_gate[0].astype(BF16), w_exp_up[0].astype(BF16), blk_exp, n_used)
    out_slots = _expert_down(hid, w_exp_down[0].astype(BF16), blk_exp, n_used)

    wsg, wsu, wsd = w_sh_gate[0].astype(BF16), w_sh_up[0].astype(BF16), w_sh_down[0].astype(BF16)
    swi = lambda accs, aux: [_silu(accs[0]) * accs[1]]
    resid = lambda accs, aux: [aux[0] + aux[1] * accs[0]]

    def finish(tag, h2g, x1g, mod_g, slots_g, gates_g):
        (hs,) = _mm(f"{tag}_sh_up", [h2g], [(0, wsg), (0, wsu)], [], swi, [BF16], tm=512, tn=512)
        (xres,) = _mm(f"{tag}_sh_down", [hs], [(0, wsd)], [x1g, mod_g["gate2"]], resid, [F32], tm=512, tn=512)
        return _combine(f"{tag}_combine", out_slots, slots_g, gates_g, xres, mod_g["gate2"], final_g[None, :])

    y_p = finish("p", h2p, x1p, mod_p, slots[:Mp * TOP_K], gates[:Mp])
    y_s = finish("s", h2s, x1s, mod_s, slots[Mp * TOP_K:], gates[Mp:])

    hd = (N_HEADS, HEAD_DIM)
    return (y_p.reshape(B, L, D), y_s.reshape(SB, ST, D),
            kp.reshape(1, B, L, *hd), vp.reshape(1, B, L, *hd), fp.reshape(1, B, L, N_HEADS),
            ks.reshape(1, SB, ST, *hd), vs.reshape(1, SB, ST, *hd), fs.reshape(1, SB, ST, N_HEADS),
            gs.reshape(1, SB, ST, GM_WIDTH))
```

```python
import functools
import math

import jax
import jax.numpy as jnp
import numpy as np
from jax import lax
from jax.experimental import pallas as pl
from jax.experimental.pallas import tpu as pltpu

F32 = jnp.float32
BF16 = jnp.bfloat16
I32 = jnp.int32

N_HEADS = 16
HEAD_DIM = 128
WIDTH_A = N_HEADS * HEAD_DIM
GM_GROUPS = 8
GM_WIDTH = 2048
GM_GROUP_DIM = GM_WIDTH // GM_GROUPS
GM_CHUNK = 128
N_EXPERTS = 64
TOP_K = 8
N_EXPERT_GROUPS = 8
TOPK_GROUPS = 4
ROUTED_SCALE = 2.5
RMS_EPS = 1e-6
LN_EPS = 1e-5

LANES = 128
NORM_TM = 256
MOE_TM = 256
COMBINE_TB = 32
VMEM_LIMIT = 56 << 20


def _cparams(sem, vmem=VMEM_LIMIT):
    return pltpu.CompilerParams(dimension_semantics=sem, vmem_limit_bytes=vmem)


def _mm(name, xs, ws, aux, epilogue, out_dtypes, *, tm, tn, x_pre=None):
    M = xs[0].shape[0]
    N = ws[0][1].shape[1]
    assert M % tm == 0 and N % tn == 0, (name, M, N, tm, tn)
    in_specs, args = [], []
    for x in xs:
        in_specs.append(pl.BlockSpec((tm, x.shape[1]), lambda i, j: (i, 0)))
        args.append(x)
    for _, w in ws:
        in_specs.append(pl.BlockSpec((w.shape[0], tn), lambda i, j: (0, j)))
        args.append(w)
    for a in aux:
        if a.shape[0] == 1:
            in_specs.append(pl.BlockSpec((1, tn), lambda i, j: (0, j)))
        else:
            assert a.shape == (M, N), (name, a.shape)
            in_specs.append(pl.BlockSpec((tm, tn), lambda i, j: (i, j)))
        args.append(a)
    n_x, n_w, n_a = len(xs), len(ws), len(aux)
    x_of_w = [xi for xi, _ in ws]

    def body(*refs):
        x_refs = refs[:n_x]
        w_refs = refs[n_x:n_x + n_w]
        a_refs = refs[n_x + n_w:n_x + n_w + n_a]
        o_refs = refs[n_x + n_w + n_a:]
        xv = []
        for r in x_refs:
            x = r[...]
            if x_pre is not None:
                x = x_pre(x)
            xv.append(x.astype(BF16))
        accs = [jnp.dot(xv[xi], w_refs[k][...].astype(BF16), preferred_element_type=F32)
                for k, xi in enumerate(x_of_w)]
        res = epilogue(accs, [r[...].astype(F32) for r in a_refs])
        for o, r in zip(o_refs, res):
            o[...] = r.astype(o.dtype)

    return pl.pallas_call(
        body,
        grid=(M // tm, N // tn),
        in_specs=in_specs,
        out_specs=[pl.BlockSpec((tm, tn), lambda i, j: (i, j)) for _ in out_dtypes],
        out_shape=[jax.ShapeDtypeStruct((M, N), d) for d in out_dtypes],
        compiler_params=_cparams(("parallel", "arbitrary")),
        name=name,
    )(*args)


def _gelu(x):
    return x * (lax.erf(x * (1.0 / math.sqrt(2.0))) + 1.0) * 0.5


def _log_sigmoid(z):
    return jnp.minimum(z, 0.0) - jnp.log1p(jnp.exp(-jnp.abs(z)))


def _silu(x):
    return x * jax.nn.sigmoid(x)


def _aux_spec(a, tm, d):
    if a.shape[0] == 1:
        return pl.BlockSpec((1, d), lambda i: (0, 0))
    return pl.BlockSpec((tm, d), lambda i: (i, 0))


def _normmod_math(x, g, scale, shift):
    y = x * lax.rsqrt(jnp.mean(x * x, axis=-1, keepdims=True) + RMS_EPS)
    return (y * g) * (1.0 + scale) + shift


def _normmod(name, x, g, scale, shift, *, tm, out_dtype):
    M, D = x.shape

    def body(x_ref, g_ref, sc_ref, sh_ref, o_ref):
        o_ref[...] = _normmod_math(x_ref[...], g_ref[...], sc_ref[...], sh_ref[...]).astype(o_ref.dtype)

    return pl.pallas_call(
        body,
        grid=(M // tm,),
        in_specs=[pl.BlockSpec((tm, D), lambda i: (i, 0)), pl.BlockSpec((1, D), lambda i: (0, 0)),
                  _aux_spec(scale, tm, D), _aux_spec(shift, tm, D)],
        out_specs=pl.BlockSpec((tm, D), lambda i: (i, 0)),
        out_shape=jax.ShapeDtypeStruct((M, D), out_dtype),
        compiler_params=_cparams(("parallel",)),
        name=name,
    )(x, g, scale, shift)


def _normmod_router(name, x, g, scale, shift, wr_hi, wr_lo, *, tm):
    M, D = x.shape
    E = wr_hi.shape[0]
    nt = (((1,), (1,)), ((), ()))

    def body(x_ref, g_ref, sc_ref, sh_ref, whi_ref, wlo_ref, h_ref, lg_ref):
        h = _normmod_math(x_ref[...], g_ref[...], sc_ref[...], sh_ref[...])
        h_ref[...] = h
        h_hi = h.astype(BF16)
        h_lo = (h - h_hi.astype(F32)).astype(BF16)
        whi = whi_ref[...]
        lg = lax.dot_general(whi, h_hi, nt, preferred_element_type=F32)
        lg = lg + lax.dot_general(wlo_ref[...], h_hi, nt, preferred_element_type=F32)
        lg = lg + lax.dot_general(whi, h_lo, nt, preferred_element_type=F32)
        lg_ref[...] = lg

    return pl.pallas_call(
        body,
        grid=(M // tm,),
        in_specs=[pl.BlockSpec((tm, D), lambda i: (i, 0)), pl.BlockSpec((1, D), lambda i: (0, 0)),
                  _aux_spec(scale, tm, D), _aux_spec(shift, tm, D),
                  pl.BlockSpec((E, D), lambda i: (0, 0)), pl.BlockSpec((E, D), lambda i: (0, 0))],
        out_specs=[pl.BlockSpec((tm, D), lambda i: (i, 0)), pl.BlockSpec((E, tm), lambda i: (0, i))],
        out_shape=[jax.ShapeDtypeStruct((M, D), F32), jax.ShapeDtypeStruct((E, M), F32)],
        compiler_params=_cparams(("parallel",)),
        name=name,
    )(x, g, scale, shift, wr_hi, wr_lo)


def _scan_lanes(x):
    lane = lax.broadcasted_iota(I32, x.shape, 1)
    d = 1
    while d < LANES:
        x = x + jnp.where(lane >= d, pltpu.roll(x, d, axis=1), 0.0)
        d *= 2
    return x


def _cumsum_lanes(name, x):
    B, R, L = x.shape
    assert L % LANES == 0
    n_chunks = L // LANES

    def body(x_ref, o_ref):
        def step(c, carry):
            off = pl.multiple_of(c * LANES, LANES)
            s = _scan_lanes(x_ref[:, pl.ds(off, LANES)]) + carry
            o_ref[:, pl.ds(off, LANES)] = s
            return s[:, LANES - 1:LANES]
        lax.fori_loop(0, n_chunks, step, jnp.zeros((R, 1), F32))

    return pl.pallas_call(
        body,
        grid=(B,),
        in_specs=[pl.BlockSpec((None, R, L), lambda b: (b, 0, 0))],
        out_specs=pl.BlockSpec((None, R, L), lambda b: (b, 0, 0)),
        out_shape=jax.ShapeDtypeStruct((B, R, L), F32),
        compiler_params=_cparams(("parallel",)),
        name=name,
    )(x)


_NT = (((1,), (1,)), ((), ()))
_SM_SCALE = HEAD_DIM ** -0.5
_EXP2_SCALE = _SM_SCALE * math.log2(math.e)


def _online_update(s, v, m_prev, l_prev, acc_prev):
    m_new = jnp.maximum(m_prev, jnp.max(s, axis=-1, keepdims=True))
    alpha = jnp.exp(m_prev - m_new)
    p = jnp.exp(s - m_new)
    l_new = alpha * l_prev + jnp.sum(p, axis=-1, keepdims=True)
    acc_new = alpha * acc_prev + jnp.dot(p.astype(BF16), v, preferred_element_type=F32)
    return m_new, l_new, acc_new


def _causal(s):
    row = lax.broadcasted_iota(I32, s.shape, 0)
    col = lax.broadcasted_iota(I32, s.shape, 1)
    return jnp.where(col <= row, s, -jnp.inf)


def _fox_augment(q, k, cum_col, *, tr):
    L = q.shape[0]
    H = cum_col.shape[0]

    def body(q_ref, k_ref, c_ref, qa_ref, ka_ref):
        c = c_ref[...] * (1.0 / _SM_SCALE)
        c1 = c.astype(BF16).astype(F32)
        r = c - c1
        c2 = r.astype(BF16).astype(F32)
        c3 = (r - c2).astype(BF16).astype(F32)
        lane = lax.broadcasted_iota(I32, (tr, LANES), 1)
        eq = jnp.where(lane == 0, c1, jnp.where(lane == 1, c2, jnp.where(lane == 2, c3,
                                                                      jnp.where(lane < 6, 1.0, 0.0))))
        ek = jnp.where(lane < 3, 1.0, jnp.where(lane == 3, -c1, jnp.where(lane == 4, -c2,
                                                                        jnp.where(lane == 5, -c3, 0.0))))
        qa_ref[:, :HEAD_DIM] = q_ref[...]
        qa_ref[:, HEAD_DIM:] = eq.astype(BF16)
        ka_ref[:, :HEAD_DIM] = k_ref[...]
        ka_ref[:, HEAD_DIM:] = ek.astype(BF16)

    return pl.pallas_call(
        body,
        grid=(L // tr, H),
        in_specs=[pl.BlockSpec((tr, HEAD_DIM), lambda i, h: (i, h)),
                  pl.BlockSpec((tr, HEAD_DIM), lambda i, h: (i, h)),
                  pl.BlockSpec((None, tr, 1), lambda i, h: (h, i, 0))],
        out_specs=[pl.BlockSpec((tr, 2 * HEAD_DIM), lambda i, h: (i, h)),
                   pl.BlockSpec((tr, 2 * HEAD_DIM), lambda i, h: (i, h))],
        out_shape=[jax.ShapeDtypeStruct((L, H * 2 * HEAD_DIM), BF16)] * 2,
        compiler_params=_cparams(("parallel", "parallel")),
        name="fox_augment",
    )(q, k, cum_col)


def _fox_prompt2(qa, ka, v, *, tb, half, rc):
    L = v.shape[0]
    H = v.shape[1] // HEAD_DIM
    nq = L // tb
    pairs = [(i, j) for i in range(nq) for j in range(i + 1)]
    qi_tab = jnp.asarray(np.array([p[0] for p in pairs], np.int32))
    ki_tab = jnp.asarray(np.array([p[1] for p in pairs], np.int32))
    n_half = tb // half

    def body(qi_ref, ki_ref, qa_ref, ka_ref, v_ref, o_ref, m_sc, l_sc, al_sc, acc_sc, s_sc, p_sc):
        t = pl.program_id(1)
        qb = qi_ref[t]
        kb = ki_ref[t]

        @pl.when(kb == 0)
        def _():
            m_sc[...] = jnp.full(m_sc.shape, -jnp.inf, F32)
            l_sc[...] = jnp.zeros(l_sc.shape, F32)
            acc_sc[...] = jnp.zeros(acc_sc.shape, F32)

        def softmax_rows(r0, s_c):
            rows = slice(r0, r0 + rc)
            nc = s_c.shape[1]
            m_prev = m_sc[rows, :]
            m_next = jnp.maximum(m_prev, jnp.max(s_c, axis=1, keepdims=True))
            m_rep = jnp.concatenate([m_next] * (nc // LANES), axis=1) if nc > LANES else m_next
            p = jnp.exp2((s_c - m_rep) * _EXP2_SCALE)
            alpha = jnp.exp2((m_prev - m_next) * _EXP2_SCALE)
            l_sc[rows, :] = alpha * l_sc[rows, :] + jnp.sum(p, axis=1, keepdims=True)
            m_sc[rows, :] = m_next
            al_sc[rows, :] = alpha
            p_sc[rows, :nc] = p.astype(BF16)

        def half_block(hf, ncols, diag):
            lo = hf * half
            hrows = slice(lo, lo + half)
            s_sc[hrows, :ncols] = lax.dot_general(qa_ref[hrows, :], ka_ref[:ncols, :], _NT,
                                                  preferred_element_type=F32)
            for ci in range(half // rc):
                r0 = lo + ci * rc
                rows = slice(r0, r0 + rc)
                if diag:
                    jt = r0 // LANES
                    nc = (jt + 1) * LANES
                    tile = s_sc[rows, jt * LANES:nc]
                    rowi = lax.broadcasted_iota(I32, (rc, LANES), 0) + (r0 % LANES)
                    coli = lax.broadcasted_iota(I32, (rc, LANES), 1)
                    tile = jnp.where(coli <= rowi, tile, -jnp.inf)
                    s_c = jnp.concatenate([s_sc[rows, :jt * LANES], tile], axis=1) if jt > 0 else tile
                    softmax_rows(r0, s_c)
                    if nc < ncols:
                        p_sc[rows, nc:ncols] = jnp.zeros((rc, ncols - nc), BF16)
                else:
                    softmax_rows(r0, s_sc[rows, :ncols])
            pv = jnp.dot(p_sc[hrows, :ncols], v_ref[:ncols, :], preferred_element_type=F32)
            acc_sc[hrows, :] = al_sc[hrows, :] * acc_sc[hrows, :] + pv

        @pl.when(kb < qb)
        def _():
            for hf in range(n_half):
                half_block(hf, tb, False)

        @pl.when(kb == qb)
        def _():
            for hf in range(n_half):
                half_block(hf, (hf + 1) * half, True)
            o_ref[...] = (acc_sc[...] / l_sc[...]).astype(o_ref.dtype)

    return pl.pallas_call(
        body,
        grid_spec=pltpu.PrefetchScalarGridSpec(
            num_scalar_prefetch=2,
            grid=(H, len(pairs)),
            in_specs=[
                pl.BlockSpec((tb, 2 * HEAD_DIM), lambda h, t, qi, ki: (qi[t], h)),
                pl.BlockSpec((tb, 2 * HEAD_DIM), lambda h, t, qi, ki: (ki[t], h)),
                pl.BlockSpec((tb, HEAD_DIM), lambda h, t, qi, ki: (ki[t], h)),
            ],
            out_specs=pl.BlockSpec((tb, HEAD_DIM), lambda h, t, qi, ki: (qi[t], h)),
            scratch_shapes=[pltpu.VMEM((tb, LANES), F32), pltpu.VMEM((tb, LANES), F32), pltpu.VMEM((tb, LANES), F32),
                            pltpu.VMEM((tb, HEAD_DIM), F32), pltpu.VMEM((tb, tb), F32), pltpu.VMEM((tb, tb), BF16)],
        ),
        out_shape=jax.ShapeDtypeStruct((L, H * HEAD_DIM), BF16),
        compiler_params=_cparams(("parallel", "arbitrary")),
        name="fox_prompt",
    )(qi_tab, ki_tab, qa, ka, v)


def _fox_sample(q, k_new, v_new, cache_k, cache_v, cq, ck_cache, ck_new, *, tk):
    B, T, W = q.shape
    P = cache_k.shape[1] // N_HEADS
    nk = P // tk

    def body(q_ref, kn_ref, vn_ref, ck_ref, cv_ref, cq_ref, cc_ref, cn_ref, o_ref, m_sc, l_sc, acc_sc):
        c = pl.program_id(1)

        @pl.when(c == 0)
        def _():
            for h in range(N_HEADS):
                sl = slice(h * HEAD_DIM, (h + 1) * HEAD_DIM)
                s = lax.dot_general(q_ref[:, sl], kn_ref[:, sl], _NT, preferred_element_type=F32)
                s = _causal(s * _SM_SCALE + (cq_ref[h] - cn_ref[h]))
                m, l, acc = _online_update(s, vn_ref[:, sl], jnp.full((T, 1), -jnp.inf, F32),
                                           jnp.zeros((T, 1), F32), jnp.zeros((T, HEAD_DIM), F32))
                m_sc[h] = m
                l_sc[h] = l
                acc_sc[h] = acc

        for h in range(N_HEADS):
            sl = slice(h * HEAD_DIM, (h + 1) * HEAD_DIM)
            kh = ck_ref[pl.ds(h, tk, stride=N_HEADS), :].astype(BF16)
            vh = cv_ref[pl.ds(h, tk, stride=N_HEADS), :].astype(BF16)
            s = lax.dot_general(q_ref[:, sl], kh, _NT, preferred_element_type=F32)
            s = s * _SM_SCALE + (cq_ref[h] - cc_ref[h])
            m, l, acc = _online_update(s, vh, m_sc[h], l_sc[h], acc_sc[h])
            m_sc[h] = m
            l_sc[h] = l
            acc_sc[h] = acc

        @pl.when(c == nk - 1)
        def _():
            for h in range(N_HEADS):
                sl = slice(h * HEAD_DIM, (h + 1) * HEAD_DIM)
                o_ref[:, sl] = (acc_sc[h] / l_sc[h]).astype(o_ref.dtype)

    return pl.pallas_call(
        body,
        grid=(B, nk),
        in_specs=[
            pl.BlockSpec((None, T, W), lambda b, c: (b, 0, 0)),
            pl.BlockSpec((None, T, W), lambda b, c: (b, 0, 0)),
            pl.BlockSpec((None, T, W), lambda b, c: (b, 0, 0)),
            pl.BlockSpec((None, tk * N_HEADS, HEAD_DIM), lambda b, c: (b, c, 0)),
            pl.BlockSpec((None, tk * N_HEADS, HEAD_DIM), lambda b, c: (b, c, 0)),
            pl.BlockSpec((None, N_HEADS, T, 1), lambda b, c: (b, 0, 0, 0)),
            pl.BlockSpec((None, N_HEADS, 1, tk), lambda b, c: (b, 0, 0, c)),
            pl.BlockSpec((None, N_HEADS, 1, T), lambda b, c: (b, 0, 0, 0)),
        ],
        out_specs=pl.BlockSpec((None, T, W), lambda b, c: (b, 0, 0)),
        out_shape=jax.ShapeDtypeStruct((B, T, W), BF16),
        scratch_shapes=[pltpu.VMEM((N_HEADS, T, 1), F32), pltpu.VMEM((N_HEADS, T, 1), F32),
                        pltpu.VMEM((N_HEADS, T, HEAD_DIM), F32)],
        compiler_params=_cparams(("parallel", "arbitrary")),
        name="fox_sample",
    )(q, k_new, v_new, cache_k, cache_v, cq, ck_cache, ck_new)


def _spatial(name, gv, u, ln_g, ln_b, ws, bs, *, lc, tm):
    M = gv.shape[0]
    n_chunks = tm // lc

    def body(gv_ref, u_ref, lg_ref, lb_ref, ws_ref, bs_ref, vb_ref, sp_ref):
        x = gv_ref[...]
        mu = jnp.mean(x, axis=-1, keepdims=True)
        xc = x - mu
        var = jnp.mean(xc * xc, axis=-1, keepdims=True)
        vb = xc * lax.rsqrt(var + LN_EPS) * lg_ref[...] + lb_ref[...]
        vb_ref[...] = vb
        row = lax.broadcasted_iota(I32, (lc, lc), 0)
        col = lax.broadcasted_iota(I32, (lc, lc), 1)
        for g in range(GM_GROUPS):
            w = jnp.where(col <= row, ws_ref[g], 0.0).astype(BF16)
            b = bs_ref[g]
            cs = slice(g * GM_GROUP_DIM, (g + 1) * GM_GROUP_DIM)
            for c in range(n_chunks):
                rs = slice(c * lc, (c + 1) * lc)
                mixed = jnp.dot(w, vb_ref[rs, cs].astype(BF16), preferred_element_type=F32) + b
                sp_ref[rs, cs] = (u_ref[rs, cs] * mixed).astype(sp_ref.dtype)

    return pl.pallas_call(
        body,
        grid=(M // tm,),
        in_specs=[pl.BlockSpec((tm, GM_WIDTH), lambda i: (i, 0)), pl.BlockSpec((tm, GM_WIDTH), lambda i: (i, 0)),
                  pl.BlockSpec((1, GM_WIDTH), lambda i: (0, 0)), pl.BlockSpec((1, GM_WIDTH), lambda i: (0, 0)),
                  pl.BlockSpec((GM_GROUPS, lc, lc), lambda i: (0, 0, 0)),
                  pl.BlockSpec((GM_GROUPS, lc, 1), lambda i: (0, 0, 0))],
        out_specs=[pl.BlockSpec((tm, GM_WIDTH), lambda i: (i, 0)), pl.BlockSpec((tm, GM_WIDTH), lambda i: (i, 0))],
        out_shape=[jax.ShapeDtypeStruct((M, GM_WIDTH), F32), jax.ShapeDtypeStruct((M, GM_WIDTH), BF16)],
        compiler_params=_cparams(("parallel",)),
        name=name,
    )(gv, u, ln_g, ln_b, ws, bs)


def _route(logits_t, bias, *, tt):
    E, M = logits_t.shape
    per_group = E // N_EXPERT_GROUPS
    neg = -jnp.inf

    def body(lg_ref, b_ref, idx_ref, gate_ref, oh_ref):
        scores = jax.nn.sigmoid(lg_ref[...])
        sel = scores + b_ref[...]
        sel3 = sel.reshape(N_EXPERT_GROUPS, per_group, tt)
        io_in = lax.broadcasted_iota(I32, sel3.shape, 1)
        m1 = jnp.max(sel3, axis=1, keepdims=True)
        first = jnp.min(jnp.where(sel3 == m1, io_in, per_group), axis=1, keepdims=True)
        m2 = jnp.max(jnp.where(io_in == first, neg, sel3), axis=1, keepdims=True)
        grp_score = (m1 + m2).reshape(N_EXPERT_GROUPS, tt)
        io_g = lax.broadcasted_iota(I32, grp_score.shape, 0)
        grp_sel = jnp.zeros(grp_score.shape, F32)
        work = grp_score
        for _ in range(TOPK_GROUPS):
            mx = jnp.max(work, axis=0, keepdims=True)
            fi = jnp.min(jnp.where(work == mx, io_g, N_EXPERT_GROUPS), axis=0, keepdims=True)
            pick = io_g == fi
            grp_sel = jnp.where(pick, 1.0, grp_sel)
            work = jnp.where(pick, neg, work)
        keep3 = jnp.broadcast_to(grp_sel.reshape(N_EXPERT_GROUPS, 1, tt), sel3.shape)
        work = jnp.where(keep3 > 0.5, sel3, neg).reshape(E, tt)
        io_e = lax.broadcasted_iota(I32, (E, tt), 0)
        onehot = jnp.zeros((E, tt), F32)
        wts = []
        for k in range(TOP_K):
            mx = jnp.max(work, axis=0, keepdims=True)
            fi = jnp.min(jnp.where(work == mx, io_e, E), axis=0, keepdims=True)
            pick = io_e == fi
            idx_ref[k:k + 1, :] = fi
            wts.append(jnp.sum(jnp.where(pick, scores, 0.0), axis=0, keepdims=True))
            onehot = jnp.where(pick, 1.0, onehot)
            work = jnp.where(pick, neg, work)
        total = wts[0]
        for k in range(1, TOP_K):
            total = total + wts[k]
        for k in range(TOP_K):
            gate_ref[k:k + 1, :] = wts[k] / total * ROUTED_SCALE
        oh_ref[...] = onehot

    return pl.pallas_call(
        body,
        grid=(M // tt,),
        in_specs=[pl.BlockSpec((E, tt), lambda i: (0, i)), pl.BlockSpec((E, 1), lambda i: (0, 0))],
        out_specs=[pl.BlockSpec((TOP_K, tt), lambda i: (0, i)), pl.BlockSpec((TOP_K, tt), lambda i: (0, i)),
                   pl.BlockSpec((E, tt), lambda i: (0, i))],
        out_shape=[jax.ShapeDtypeStruct((TOP_K, M), I32), jax.ShapeDtypeStruct((TOP_K, M), F32),
                   jax.ShapeDtypeStruct((E, M), F32)],
        compiler_params=_cparams(("parallel",)),
        name="moe_route",
    )(logits_t, bias)


def _slots(idx_t, pos_t, start, *, tt):
    K, M = idx_t.shape
    E = pos_t.shape[0]

    def body(idx_ref, pos_ref, st_ref, o_ref):
        base = pos_ref[...] - 1.0 + st_ref[...]
        io_e = lax.broadcasted_iota(I32, (E, tt), 0)
        for k in range(K):
            pick = io_e == idx_ref[k:k + 1, :]
            o_ref[k:k + 1, :] = jnp.sum(jnp.where(pick, base, 0.0), axis=0, keepdims=True).astype(I32)

    return pl.pallas_call(
        body,
        grid=(M // tt,),
        in_specs=[pl.BlockSpec((K, tt), lambda i: (0, i)), pl.BlockSpec((E, tt), lambda i: (0, i)),
                  pl.BlockSpec((E, 1), lambda i: (0, 0))],
        out_specs=pl.BlockSpec((K, tt), lambda i: (0, i)),
        out_shape=jax.ShapeDtypeStruct((K, M), I32),
        compiler_params=_cparams(("parallel",)),
        name="moe_slots",
    )(idx_t, pos_t, start)


def _expert_up(h, slot_tok, wg, wu, blk_exp, n_used):
    M, D = h.shape
    S = slot_tok.shape[0]
    DE = wg.shape[2]
    nb = S // MOE_TM

    def body(be_ref, nu_ref, st_ref, h_ref, wg_ref, wu_ref, o_ref, xbuf, sem):
        b = pl.program_id(0)
        nu = nu_ref[0]

        def row_copy(tok, buf, r):
            return pltpu.make_async_copy(h_ref.at[pl.ds(tok, 1)], xbuf.at[buf, pl.ds(r, 1)], sem.at[buf])

        def gather(blk, buf):
            for r in range(MOE_TM):
                row_copy(st_ref[blk * MOE_TM + r], buf, r).start()

        @pl.when(b == 0)
        def _():
            gather(0, 0)

        @pl.when(b + 1 < nu)
        def _():
            gather(b + 1, (b + 1) % 2)

        @pl.when(b < nu)
        def _():
            buf = b % 2

            for _ in range(MOE_TM):
                row_copy(0, buf, 0).wait()
            x = xbuf[buf].astype(BF16)
            g = jnp.dot(x, wg_ref[...], preferred_element_type=F32)
            u = jnp.dot(x, wu_ref[...], preferred_element_type=F32)
            o_ref[...] = (_silu(g) * u).astype(o_ref.dtype)

        @pl.when(b >= nu)
        def _():
            o_ref[...] = jnp.zeros(o_ref.shape, o_ref.dtype)

    return pl.pallas_call(
        body,
        grid_spec=pltpu.PrefetchScalarGridSpec(
            num_scalar_prefetch=3,
            grid=(nb,),
            in_specs=[pl.BlockSpec(memory_space=pl.ANY),
                      pl.BlockSpec((None, D, DE), lambda b, be, nu, st: (be[b], 0, 0)),
                      pl.BlockSpec((None, D, DE), lambda b, be, nu, st: (be[b], 0, 0))],
            out_specs=pl.BlockSpec((MOE_TM, DE), lambda b, be, nu, st: (b, 0)),
            scratch_shapes=[pltpu.VMEM((2, MOE_TM, D), F32), pltpu.SemaphoreType.DMA((2,))],
        ),
        out_shape=jax.ShapeDtypeStruct((S, DE), BF16),
        compiler_params=_cparams(("arbitrary",)),
        name="moe_expert_up",
    )(blk_exp, n_used, slot_tok, h, wg, wu)


def _expert_down(hid, wd, blk_exp, n_used):
    S, DE = hid.shape
    D = wd.shape[2]
    nb = S // MOE_TM

    def body(be_ref, nu_ref, h_ref, wd_ref, o_ref):
        @pl.when(pl.program_id(0) < nu_ref[0])
        def _():
            o_ref[...] = jnp.dot(h_ref[...], wd_ref[...], preferred_element_type=F32)

        @pl.when(pl.program_id(0) >= nu_ref[0])
        def _():
            o_ref[...] = jnp.zeros(o_ref.shape, o_ref.dtype)

    def row(b, be, nu):
        return (jnp.minimum(b, nu[0] - 1), 0)

    return pl.pallas_call(
        body,
        grid_spec=pltpu.PrefetchScalarGridSpec(
            num_scalar_prefetch=2,
            grid=(nb,),
            in_specs=[pl.BlockSpec((MOE_TM, DE), row),
                      pl.BlockSpec((None, DE, D), lambda b, be, nu: (be[b], 0, 0))],
            out_specs=pl.BlockSpec((MOE_TM, D), lambda b, be, nu: (b, 0)),
        ),
        out_shape=jax.ShapeDtypeStruct((S, D), F32),
        compiler_params=_cparams(("arbitrary",)),
        name="moe_expert_down",
    )(blk_exp, n_used, hid, wd)


def _combine(name, out_slots, slots, gates, xres, gate2, final_g):
    M, D = xres.shape
    tb = COMBINE_TB
    nsteps = M // tb

    def body(slots_ref, src_ref, g_ref, x_ref, g2_ref, fg_ref, o_ref, buf, sem):
        i = pl.program_id(0)

        def row_copy(s, b, k, r):
            return pltpu.make_async_copy(src_ref.at[pl.ds(s, 1)], buf.at[b, k, pl.ds(r, 1)], sem.at[b])

        def issue(blk, b):
            for r in range(tb):
                for k in range(TOP_K):
                    row_copy(slots_ref[(blk * tb + r) * TOP_K + k], b, k, r).start()

        @pl.when(i == 0)
        def _():
            issue(0, 0)

        @pl.when(i + 1 < nsteps)
        def _():
            issue(i + 1, (i + 1) % 2)

        b = i % 2

        for _ in range(tb * TOP_K):
            row_copy(0, b, 0, 0).wait()

        g = g_ref[...]
        routed = g[:, 0:1] * buf[b, 0]
        for k in range(1, TOP_K):
            routed = routed + g[:, k:k + 1] * buf[b, k]
        x2 = x_ref[...] + g2_ref[...] * routed
        y = x2 * lax.rsqrt(jnp.mean(x2 * x2, axis=-1, keepdims=True) + RMS_EPS)
        o_ref[...] = y * fg_ref[...]

    return pl.pallas_call(
        body,
        grid_spec=pltpu.PrefetchScalarGridSpec(
            num_scalar_prefetch=1,
            grid=(nsteps,),
            in_specs=[pl.BlockSpec(memory_space=pl.ANY),
                      pl.BlockSpec((tb, TOP_K), lambda i, s: (i, 0)),
                      pl.BlockSpec((tb, D), lambda i, s: (i, 0)),
                      (pl.BlockSpec((1, D), lambda i, s: (0, 0)) if gate2.shape[0] == 1
                       else pl.BlockSpec((tb, D), lambda i, s: (i, 0))),
                      pl.BlockSpec((1, D), lambda i, s: (0, 0))],
            out_specs=pl.BlockSpec((tb, D), lambda i, s: (i, 0)),
            scratch_shapes=[pltpu.VMEM((2, TOP_K, tb, D), F32), pltpu.SemaphoreType.DMA((2,))],
        ),
        out_shape=jax.ShapeDtypeStruct((M, D), F32),
        compiler_params=_cparams(("arbitrary",)),
        name=name,
    )(slots, out_slots, gates, xres, gate2, final_g)


def _mixer_sublayer(tag, x, mod, w, attend, *, tm, lc):
    M, D = x.shape
    h = _normmod(f"{tag}_norm1", x, w["norm1_g"], mod["scale1"], mod["shift1"], tm=NORM_TM, out_dtype=BF16)
    ident = lambda accs, aux: [accs[0]]
    both = lambda accs, aux: [accs[0], accs[0]]
    (q,) = _mm(f"{tag}_q", [h], [(0, w["w_q"])], [], ident, [BF16], tm=tm, tn=512)
    k32, k16 = _mm(f"{tag}_k", [h], [(0, w["w_k"])], [], both, [F32, BF16], tm=tm, tn=512)
    v32, v16 = _mm(f"{tag}_v", [h], [(0, w["w_v"])], [], both, [F32, BF16], tm=tm, tn=512)
    (logf_pad,) = _mm(f"{tag}_f", [h], [(0, w["w_f"])], [w["b_f"]],
                      lambda accs, aux: [_log_sigmoid(accs[0] + aux[0])], [F32], tm=tm, tn=LANES)
    logf = logf_pad[:, :N_HEADS]
    (u,) = _mm(f"{tag}_u", [h], [(0, w["w_u"])], [], lambda accs, aux: [_gelu(accs[0])], [F32], tm=tm, tn=512)
    (gv,) = _mm(f"{tag}_vb", [h], [(0, w["w_vb"])], [], lambda accs, aux: [_gelu(accs[0])], [F32], tm=tm, tn=512)
    sig = lambda accs, aux: [jax.nn.sigmoid(accs[0])]
    (ga,) = _mm(f"{tag}_ga", [h], [(0, w["w_ga"])], [], sig, [BF16], tm=tm, tn=512)
    (gb,) = _mm(f"{tag}_gb", [h], [(0, w["w_gb"])], [], sig, [BF16], tm=tm, tn=512)

    attn = attend(q, k16, v16, logf)
    vb, spatial = _spatial(f"{tag}_spatial", gv, u, w["gm_ln_g"], w["gm_ln_b"], w["gm_ws"][:, :lc, :lc],
                           w["gm_bs"][:, :lc, None], lc=lc, tm=tm)
    (merged,) = _mm(f"{tag}_merge", [attn, spatial], [(0, w["w_a"]), (1, w["w_b"])], [ga, gb],
                    lambda accs, aux: [aux[0] * accs[0] + aux[1] * accs[1]], [BF16], tm=tm, tn=512)
    (x1,) = _mm(f"{tag}_out", [merged], [(0, w["w_out"])], [x, mod["gate1"]],
                lambda accs, aux: [aux[0] + aux[1] * accs[0]], [F32], tm=tm, tn=512)
    return x1, k32, v32, logf, vb


def _pad_lanes(x):
    pad = (-x.shape[-1]) % LANES
    return jnp.pad(x, [(0, 0)] * (x.ndim - 1) + [(0, pad)]) if pad else x


def kernel(x_prompt, x_sample, cache_k, cache_v, cache_logf, c_prompt, c_sample, ada_w, ada_b, norm1_g, w_in, b_forget, gm_ln_g, gm_ln_b, gm_ws, gm_bs, w_branch_a, w_branch_b, w_out, norm2_g, w_router, router_bias, w_exp_gate, w_exp_up, w_exp_down, w_sh_gate, w_sh_up, w_sh_down, final_g):
    assert ada_w.shape[0] == 1, "single trunk layer"
    B, L, D = x_prompt.shape
    SB, ST, _ = x_sample.shape
    assert B == 1
    P = cache_k.shape[2]
    Mp, Ms = B * L, SB * ST
    M = Mp + Ms

    wi = w_in[0]
    c_q, c_k, c_v, c_f = 0, WIDTH_A, 2 * WIDTH_A, 3 * WIDTH_A
    c_u = c_f + N_HEADS
    c_vb = c_u + GM_WIDTH
    c_ga = c_vb + GM_WIDTH
    c_gb = c_ga + D
    w = {
        "norm1_g": norm1_g, "gm_ln_g": gm_ln_g, "gm_ln_b": gm_ln_b, "gm_ws": gm_ws[0], "gm_bs": gm_bs[0],
        "w_q": wi[:, c_q:c_k].astype(BF16), "w_k": wi[:, c_k:c_v].astype(BF16), "w_v": wi[:, c_v:c_f].astype(BF16),
        "w_f": _pad_lanes(wi[:, c_f:c_u]).astype(BF16), "b_f": _pad_lanes(b_forget),
        "w_u": wi[:, c_u:c_vb].astype(BF16), "w_vb": wi[:, c_vb:c_ga].astype(BF16),
        "w_ga": wi[:, c_ga:c_gb].astype(BF16), "w_gb": wi[:, c_gb:].astype(BF16),
        "w_a": w_branch_a[0].astype(BF16), "w_b": w_branch_b[0].astype(BF16), "w_out": w_out[0].astype(BF16),
    }

    c_all = jnp.concatenate([c_prompt, c_sample], axis=0)
    n_c = c_all.shape[0]
    c_all = jnp.pad(c_all, ((0, (-n_c) % 32), (0, 0)))
    (mod,) = _mm("ada_mod", [c_all], [(0, ada_w[0])], [ada_b], lambda accs, aux: [accs[0] + aux[0]], [F32],
                 tm=c_all.shape[0], tn=512, x_pre=_silu)
    names = ["shift1", "scale1", "gate1", "shift2", "scale2", "gate2"]
    mod_p = {n: mod[0:B, i * D:(i + 1) * D] for i, n in enumerate(names)}
    mod_s = {n: jnp.broadcast_to(mod[B:B + SB, None, i * D:(i + 1) * D], (SB, ST, D)).reshape(Ms, D)
             for i, n in enumerate(names)}

    def attend_prompt(q, k16, v16, logf):
        cum_t = _cumsum_lanes("p_cumsum", jnp.transpose(logf)[None])[0]
        qa, ka = _fox_augment(q, k16, cum_t[:, :, None], tr=512)
        return _fox_prompt2(qa, ka, v16, tb=1024, half=512, rc=32)

    def attend_sample(q, k16, v16, logf):
        lf = jnp.concatenate([cache_logf[0].astype(F32), logf.reshape(SB, ST, N_HEADS)], axis=1)
        lf_t = _pad_lanes(jnp.transpose(lf, (0, 2, 1)))
        cum_t = _cumsum_lanes("s_cumsum", lf_t.reshape(1, SB * N_HEADS, -1)).reshape(lf_t.shape)
        ck_cache = cum_t[:, :, None, :P]
        ck_new = cum_t[:, :, None, P:P + ST]
        cq = cum_t[:, :, P:P + ST, None]
        o = _fox_sample(q.reshape(SB, ST, WIDTH_A), k16.reshape(SB, ST, WIDTH_A), v16.reshape(SB, ST, WIDTH_A),
                        cache_k[0].reshape(SB, P * N_HEADS, HEAD_DIM), cache_v[0].reshape(SB, P * N_HEADS, HEAD_DIM),
                        cq, ck_cache, ck_new, tk=512)
        return o.reshape(Ms, WIDTH_A)

    x1p, kp, vp, fp, _ = _mixer_sublayer("p", x_prompt.reshape(Mp, D), mod_p, w, attend_prompt, tm=512,
                                         lc=min(L, GM_CHUNK))
    x1s, ks, vs, fs, gs = _mixer_sublayer("s", x_sample.reshape(Ms, D), mod_s, w, attend_sample, tm=512,
                                          lc=min(ST, GM_CHUNK))

    wr = jnp.transpose(w_router[0])
    wr_hi = wr.astype(BF16)
    wr_lo = (wr - wr_hi.astype(F32)).astype(BF16)
    h2p, lgp = _normmod_router("p_norm2", x1p, norm2_g, mod_p["scale2"], mod_p["shift2"], wr_hi, wr_lo, tm=NORM_TM)
    h2s, lgs = _normmod_router("s_norm2", x1s, norm2_g, mod_s["scale2"], mod_s["shift2"], wr_hi, wr_lo, tm=NORM_TM)
    h2 = jnp.concatenate([h2p, h2s], axis=0)
    logits_t = jnp.concatenate([lgp, lgs], axis=1)

    idx_t, gates_t, onehot_t = _route(logits_t, jnp.transpose(router_bias).astype(F32), tt=512)
    pos_t = _cumsum_lanes("moe_rank", onehot_t[None])[0]
    counts = pos_t[:, M - 1].astype(I32)
    padded = (counts + MOE_TM - 1) // MOE_TM * MOE_TM
    pad_end = jnp.cumsum(padded)
    pad_start = pad_end - padded
    n_blocks = -(-(M * TOP_K + N_EXPERTS * (MOE_TM - 1)) // MOE_TM)
    n_slots = n_blocks * MOE_TM
    n_used = (pad_end[N_EXPERTS - 1] // MOE_TM).astype(I32).reshape(1)
    blk_ids = jnp.minimum(jnp.arange(n_blocks, dtype=I32), n_used[0] - 1)
    blk_exp = jnp.sum((pad_end[None, :] <= (blk_ids * MOE_TM)[:, None]).astype(I32), axis=1)
    blk_exp = jnp.minimum(blk_exp, N_EXPERTS - 1)
    slot_t = _slots(idx_t, pos_t, pad_start.astype(F32)[:, None], tt=512)
    slots = jnp.transpose(slot_t).reshape(M * TOP_K)
    gates = jnp.transpose(gates_t)
    slot_tok = jnp.zeros((n_slots,), I32).at[slots].set(jnp.arange(M * TOP_K, dtype=I32) // TOP_K)

    hid = _expert_up(h2, slot_tok, w_exp_gate[0].astype(BF16), w_exp_up[0].astype(BF16), blk_exp, n_used)
    out_slots = _expert_down(hid, w_exp_down[0].astype(BF16), blk_exp, n_used)

    wsg, wsu, wsd = w_sh_gate[0].astype(BF16), w_sh_up[0].astype(BF16), w_sh_down[0].astype(BF16)
    swi = lambda accs, aux: [_silu(accs[0]) * accs[1]]
    resid = lambda accs, aux: [aux[0] + aux[1] * accs[0]]

    def finish(tag, h2g, x1g, mod_g, slots_g, gates_g):
        (hs,) = _mm(f"{tag}_sh_up", [h2g], [(0, wsg), (0, wsu)], [], swi, [BF16], tm=512, tn=512)
        (xres,) = _mm(f"{tag}_sh_down", [hs], [(0, wsd)], [x1g, mod_g["gate2"]], resid, [F32], tm=512, tn=512)
        return _combine(f"{tag}_combine", out_slots, slots_g, gates_g, xres, mod_g["gate2"], final_g[None, :])

    y_p = finish("p", h2p, x1p, mod_p, slots[:Mp * TOP_K], gates[:Mp])
    y_s = finish("s", h2s, x1s, mod_s, slots[Mp * TOP_K:], gates[Mp:])

    hd = (N_HEADS, HEAD_DIM)
    return (y_p.reshape(B, L, D), y_s.reshape(SB, ST, D),
            kp.reshape(1, B, L, *hd), vp.reshape(1, B, L, *hd), fp.reshape(1, B, L, N_HEADS),
            ks.reshape(1, SB, ST, *hd), vs.reshape(1, SB, ST, *hd), fs.reshape(1, SB, ST, N_HEADS),
            gs.reshape(1, SB, ST, GM_WIDTH))
```

```python
import functools
import math

import jax
import jax.numpy as jnp
import numpy as np
from jax import lax
from jax.experimental import pallas as pl
from jax.experimental.pallas import tpu as pltpu

F32 = jnp.float32
BF16 = jnp.bfloat16
I32 = jnp.int32

N_HEADS = 16
HEAD_DIM = 128
WIDTH_A = N_HEADS * HEAD_DIM
GM_GROUPS = 8
GM_WIDTH = 2048
GM_GROUP_DIM = GM_WIDTH // GM_GROUPS
GM_CHUNK = 128
N_EXPERTS = 64
TOP_K = 8
N_EXPERT_GROUPS = 8
TOPK_GROUPS = 4
ROUTED_SCALE = 2.5
RMS_EPS = 1e-6
LN_EPS = 1e-5

LANES = 128
SUBLANES = 8
NORM_TM = 256
MOE_TM = 256
COMBINE_TB = 32
EXP_NP = 16
EXP_NS = 4
VMEM_LIMIT = 56 << 20
EXPERT_UP_VMEM = 60 << 20


def _cparams(sem, vmem=VMEM_LIMIT):
    return pltpu.CompilerParams(dimension_semantics=sem, vmem_limit_bytes=vmem)


def _mm(name, xs, ws, aux, epilogue, out_dtypes, *, tm, tn, x_pre=None):
    M = xs[0].shape[0]
    N = ws[0][1].shape[1]
    assert M % tm == 0 and N % tn == 0, (name, M, N, tm, tn)
    in_specs, args = [], []
    for x in xs:
        in_specs.append(pl.BlockSpec((tm, x.shape[1]), lambda i, j: (i, 0)))
        args.append(x)
    for _, w in ws:
        in_specs.append(pl.BlockSpec((w.shape[0], tn), lambda i, j: (0, j)))
        args.append(w)
    for a in aux:
        if a.shape[0] == 1:
            in_specs.append(pl.BlockSpec((1, tn), lambda i, j: (0, j)))
        else:
            assert a.shape == (M, N), (name, a.shape)
            in_specs.append(pl.BlockSpec((tm, tn), lambda i, j: (i, j)))
        args.append(a)
    n_x, n_w, n_a = len(xs), len(ws), len(aux)
    x_of_w = [xi for xi, _ in ws]

    def body(*refs):
        x_refs = refs[:n_x]
        w_refs = refs[n_x:n_x + n_w]
        a_refs = refs[n_x + n_w:n_x + n_w + n_a]
        o_refs = refs[n_x + n_w + n_a:]
        xv = []
        for r in x_refs:
            x = r[...]
            if x_pre is not None:
                x = x_pre(x)
            xv.append(x.astype(BF16))
        accs = [jnp.dot(xv[xi], w_refs[k][...].astype(BF16), preferred_element_type=F32)
                for k, xi in enumerate(x_of_w)]
        res = epilogue(accs, [r[...].astype(F32) for r in a_refs])
        for o, r in zip(o_refs, res):
            o[...] = r.astype(o.dtype)

    return pl.pallas_call(
        body,
        grid=(M // tm, N // tn),
        in_specs=in_specs,
        out_specs=[pl.BlockSpec((tm, tn), lambda i, j: (i, j)) for _ in out_dtypes],
        out_shape=[jax.ShapeDtypeStruct((M, N), d) for d in out_dtypes],
        compiler_params=_cparams(("parallel", "arbitrary")),
        name=name,
    )(*args)


def _gelu(x):
    return x * (lax.erf(x * (1.0 / math.sqrt(2.0))) + 1.0) * 0.5


def _log_sigmoid(z):
    return jnp.minimum(z, 0.0) - jnp.log1p(jnp.exp(-jnp.abs(z)))


def _silu(x):
    return x * jax.nn.sigmoid(x)


def _aux_spec(a, tm, d):
    if a.shape[0] == 1:
        return pl.BlockSpec((1, d), lambda i: (0, 0))
    return pl.BlockSpec((tm, d), lambda i: (i, 0))


def _normmod_math(x, g, scale, shift):
    y = x * lax.rsqrt(jnp.mean(x * x, axis=-1, keepdims=True) + RMS_EPS)
    return (y * g) * (1.0 + scale) + shift


def _normmod(name, x, g, scale, shift, *, tm, out_dtype):
    M, D = x.shape

    def body(x_ref, g_ref, sc_ref, sh_ref, o_ref):
        o_ref[...] = _normmod_math(x_ref[...], g_ref[...], sc_ref[...], sh_ref[...]).astype(o_ref.dtype)

    return pl.pallas_call(
        body,
        grid=(M // tm,),
        in_specs=[pl.BlockSpec((tm, D), lambda i: (i, 0)), pl.BlockSpec((1, D), lambda i: (0, 0)),
                  _aux_spec(scale, tm, D), _aux_spec(shift, tm, D)],
        out_specs=pl.BlockSpec((tm, D), lambda i: (i, 0)),
        out_shape=jax.ShapeDtypeStruct((M, D), out_dtype),
        compiler_params=_cparams(("parallel",)),
        name=name,
    )(x, g, scale, shift)


def _normmod_router(name, x, g, scale, shift, wr_hi, wr_lo, *, tm):
    M, D = x.shape
    E = wr_hi.shape[0]
    nt = (((1,), (1,)), ((), ()))

    def body(x_ref, g_ref, sc_ref, sh_ref, whi_ref, wlo_ref, h_ref, lg_ref):
        h = _normmod_math(x_ref[...], g_ref[...], sc_ref[...], sh_ref[...])
        h_ref[...] = h
        h_hi = h.astype(BF16)
        h_lo = (h - h_hi.astype(F32)).astype(BF16)
        whi = whi_ref[...]
        lg = lax.dot_general(whi, h_hi, nt, preferred_element_type=F32)
        lg = lg + lax.dot_general(wlo_ref[...], h_hi, nt, preferred_element_type=F32)
        lg = lg + lax.dot_general(whi, h_lo, nt, preferred_element_type=F32)
        lg_ref[...] = lg

    return pl.pallas_call(
        body,
        grid=(M // tm,),
        in_specs=[pl.BlockSpec((tm, D), lambda i: (i, 0)), pl.BlockSpec((1, D), lambda i: (0, 0)),
                  _aux_spec(scale, tm, D), _aux_spec(shift, tm, D),
                  pl.BlockSpec((E, D), lambda i: (0, 0)), pl.BlockSpec((E, D), lambda i: (0, 0))],
        out_specs=[pl.BlockSpec((tm, D), lambda i: (i, 0)), pl.BlockSpec((E, tm), lambda i: (0, i))],
        out_shape=[jax.ShapeDtypeStruct((M, D), F32), jax.ShapeDtypeStruct((E, M), F32)],
        compiler_params=_cparams(("parallel",)),
        name=name,
    )(x, g, scale, shift, wr_hi, wr_lo)


def _scan_lanes(x):
    lane = lax.broadcasted_iota(I32, x.shape, 1)
    d = 1
    while d < LANES:
        x = x + jnp.where(lane >= d, pltpu.roll(x, d, axis=1), 0.0)
        d *= 2
    return x


def _cumsum_lanes(name, x):
    B, R, L = x.shape
    assert L % LANES == 0
    n_chunks = L // LANES

    def body(x_ref, o_ref):
        def step(c, carry):
            off = pl.multiple_of(c * LANES, LANES)
            s = _scan_lanes(x_ref[:, pl.ds(off, LANES)]) + carry
            o_ref[:, pl.ds(off, LANES)] = s
            return s[:, LANES - 1:LANES]
        lax.fori_loop(0, n_chunks, step, jnp.zeros((R, 1), F32))

    return pl.pallas_call(
        body,
        grid=(B,),
        in_specs=[pl.BlockSpec((None, R, L), lambda b: (b, 0, 0))],
        out_specs=pl.BlockSpec((None, R, L), lambda b: (b, 0, 0)),
        out_shape=jax.ShapeDtypeStruct((B, R, L), F32),
        compiler_params=_cparams(("parallel",)),
        name=name,
    )(x)


_NT = (((1,), (1,)), ((), ()))
_SM_SCALE = HEAD_DIM ** -0.5
_EXP2_SCALE = _SM_SCALE * math.log2(math.e)


def _online_update(s, v, m_prev, l_prev, acc_prev):
    m_new = jnp.maximum(m_prev, jnp.max(s, axis=-1, keepdims=True))
    alpha = jnp.exp(m_prev - m_new)
    p = jnp.exp(s - m_new)
    l_new = alpha * l_prev + jnp.sum(p, axis=-1, keepdims=True)
    acc_new = alpha * acc_prev + jnp.dot(p.astype(BF16), v, preferred_element_type=F32)
    return m_new, l_new, acc_new


def _causal(s):
    row = lax.broadcasted_iota(I32, s.shape, 0)
    col = lax.broadcasted_iota(I32, s.shape, 1)
    return jnp.where(col <= row, s, -jnp.inf)


def _fox_augment(q, k, cum_col, *, tr):
    L = q.shape[0]
    H = cum_col.shape[0]

    def body(q_ref, k_ref, c_ref, qa_ref, ka_ref):
        c = c_ref[...] * (1.0 / _SM_SCALE)
        c1 = c.astype(BF16).astype(F32)
        r = c - c1
        c2 = r.astype(BF16).astype(F32)
        c3 = (r - c2).astype(BF16).astype(F32)
        lane = lax.broadcasted_iota(I32, (tr, LANES), 1)
        eq = jnp.where(lane == 0, c1, jnp.where(lane == 1, c2, jnp.where(lane == 2, c3,
                                                                      jnp.where(lane < 6, 1.0, 0.0))))
        ek = jnp.where(lane < 3, 1.0, jnp.where(lane == 3, -c1, jnp.where(lane == 4, -c2,
                                                                        jnp.where(lane == 5, -c3, 0.0))))
        qa_ref[:, :HEAD_DIM] = q_ref[...]
        qa_ref[:, HEAD_DIM:] = eq.astype(BF16)
        ka_ref[:, :HEAD_DIM] = k_ref[...]
        ka_ref[:, HEAD_DIM:] = ek.astype(BF16)

    return pl.pallas_call(
        body,
        grid=(L // tr, H),
        in_specs=[pl.BlockSpec((tr, HEAD_DIM), lambda i, h: (i, h)),
                  pl.BlockSpec((tr, HEAD_DIM), lambda i, h: (i, h)),
                  pl.BlockSpec((None, tr, 1), lambda i, h: (h, i, 0))],
        out_specs=[pl.BlockSpec((tr, 2 * HEAD_DIM), lambda i, h: (i, h)),
                   pl.BlockSpec((tr, 2 * HEAD_DIM), lambda i, h: (i, h))],
        out_shape=[jax.ShapeDtypeStruct((L, H * 2 * HEAD_DIM), BF16)] * 2,
        compiler_params=_cparams(("parallel", "parallel")),
        name="fox_augment",
    )(q, k, cum_col)


def _fox_prompt2(qa, ka, v, *, tb, half, rc):
    L = v.shape[0]
    H = v.shape[1] // HEAD_DIM
    nq = L // tb
    pairs = [(i, j) for i in range(nq) for j in range(i + 1)]
    qi_tab = jnp.asarray(np.array([p[0] for p in pairs], np.int32))
    ki_tab = jnp.asarray(np.array([p[1] for p in pairs], np.int32))
    n_half = tb // half

    def body(qi_ref, ki_ref, qa_ref, ka_ref, v_ref, o_ref, m_sc, l_sc, al_sc, acc_sc, s_sc, p_sc):
        t = pl.program_id(1)
        qb = qi_ref[t]
        kb = ki_ref[t]

        @pl.when(kb == 0)
        def _():
            m_sc[...] = jnp.full(m_sc.shape, -jnp.inf, F32)
            l_sc[...] = jnp.zeros(l_sc.shape, F32)
            acc_sc[...] = jnp.zeros(acc_sc.shape, F32)

        def softmax_rows(r0, s_c):
            rows = slice(r0, r0 + rc)
            nc = s_c.shape[1]
            m_prev = m_sc[rows, :]
            m_next = jnp.maximum(m_prev, jnp.max(s_c, axis=1, keepdims=True))
            m_rep = jnp.concatenate([m_next] * (nc // LANES), axis=1) if nc > LANES else m_next
            p = jnp.exp2((s_c - m_rep) * _EXP2_SCALE)
            alpha = jnp.exp2((m_prev - m_next) * _EXP2_SCALE)
            l_sc[rows, :] = alpha * l_sc[rows, :] + jnp.sum(p, axis=1, keepdims=True)
            m_sc[rows, :] = m_next
            al_sc[rows, :] = alpha
            p_sc[rows, :nc] = p.astype(BF16)

        def half_block(hf, ncols, diag):
            lo = hf * half
            hrows = slice(lo, lo + half)
            s_sc[hrows, :ncols] = lax.dot_general(qa_ref[hrows, :], ka_ref[:ncols, :], _NT,
                                                  preferred_element_type=F32)
            for ci in range(half // rc):
                r0 = lo + ci * rc
                rows = slice(r0, r0 + rc)
                if diag:
                    jt = r0 // LANES
                    nc = (jt + 1) * LANES
                    tile = s_sc[rows, jt * LANES:nc]
                    rowi = lax.broadcasted_iota(I32, (rc, LANES), 0) + (r0 % LANES)
                    coli = lax.broadcasted_iota(I32, (rc, LANES), 1)
                    tile = jnp.where(coli <= rowi, tile, -jnp.inf)
                    s_c = jnp.concatenate([s_sc[rows, :jt * LANES], tile], axis=1) if jt > 0 else tile
                    softmax_rows(r0, s_c)
                    if nc < ncols:
                        p_sc[rows, nc:ncols] = jnp.zeros((rc, ncols - nc), BF16)
                else:
                    softmax_rows(r0, s_sc[rows, :ncols])
            pv = jnp.dot(p_sc[hrows, :ncols], v_ref[:ncols, :], preferred_element_type=F32)
            acc_sc[hrows, :] = al_sc[hrows, :] * acc_sc[hrows, :] + pv

        @pl.when(kb < qb)
        def _():
            for hf in range(n_half):
                half_block(hf, tb, False)

        @pl.when(kb == qb)
        def _():
            for hf in range(n_half):
                half_block(hf, (hf + 1) * half, True)
            o_ref[...] = (acc_sc[...] / l_sc[...]).astype(o_ref.dtype)

    return pl.pallas_call(
        body,
        grid_spec=pltpu.PrefetchScalarGridSpec(
            num_scalar_prefetch=2,
            grid=(H, len(pairs)),
            in_specs=[
                pl.BlockSpec((tb, 2 * HEAD_DIM), lambda h, t, qi, ki: (qi[t], h)),
                pl.BlockSpec((tb, 2 * HEAD_DIM), lambda h, t, qi, ki: (ki[t], h)),
                pl.BlockSpec((tb, HEAD_DIM), lambda h, t, qi, ki: (ki[t], h)),
            ],
            out_specs=pl.BlockSpec((tb, HEAD_DIM), lambda h, t, qi, ki: (qi[t], h)),
            scratch_shapes=[pltpu.VMEM((tb, LANES), F32), pltpu.VMEM((tb, LANES), F32), pltpu.VMEM((tb, LANES), F32),
                            pltpu.VMEM((tb, HEAD_DIM), F32), pltpu.VMEM((tb, tb), F32), pltpu.VMEM((tb, tb), BF16)],
        ),
        out_shape=jax.ShapeDtypeStruct((L, H * HEAD_DIM), BF16),
        compiler_params=_cparams(("parallel", "arbitrary")),
        name="fox_prompt",
    )(qi_tab, ki_tab, qa, ka, v)


def _fox_sample(qg, kn, vn, cache_k, cache_v, cq, ck_cache, ck_new, *, tk):
    B, G, R, _ = qg.shape
    T = R // SUBLANES
    P = cache_k.shape[1]
    nk = P // tk
    C = tk * SUBLANES

    def body(q_ref, kn_ref, vn_ref, ck_ref, cv_ref, cq_ref, cc_ref, cn_ref, o_ref, m_sc, l_sc, acc_sc):
        c = pl.program_id(1)

        def update(g, s, v):
            m, l, acc = _online_update(s, v, m_sc[g], l_sc[g], acc_sc[g])
            m_sc[g] = m
            l_sc[g] = l
            acc_sc[g] = acc

        @pl.when(c == 0)
        def _():
            col = lax.broadcasted_iota(I32, (R, R), 1)
            row = lax.broadcasted_iota(I32, (R, R), 0)
            keep = (col // T == row // T) & (col % T <= row % T)
            for g in range(G):
                m_sc[g] = jnp.full((R, 1), -jnp.inf, F32)
                l_sc[g] = jnp.zeros((R, 1), F32)
                acc_sc[g] = jnp.zeros((R, HEAD_DIM), F32)
                s = lax.dot_general(q_ref[g], kn_ref[g], _NT, preferred_element_type=F32)
                s = s * _SM_SCALE + (cq_ref[g] - cn_ref[g])
                update(g, jnp.where(keep, s, -jnp.inf), vn_ref[g])

        row_head = lax.broadcasted_iota(I32, (R, LANES), 0) // T
        lane_head = lax.broadcasted_iota(I32, (R, LANES), 1) % SUBLANES
        keep = jnp.concatenate([lane_head == row_head] * (C // LANES), axis=1)
        for g in range(G):
            hs = slice(g * SUBLANES, (g + 1) * SUBLANES)
            kg = ck_ref[:, hs, :].reshape(C, HEAD_DIM).astype(BF16)
            vg = cv_ref[:, hs, :].reshape(C, HEAD_DIM).astype(BF16)
            s = lax.dot_general(q_ref[g], kg, _NT, preferred_element_type=F32)
            s = s * _SM_SCALE + (cq_ref[g] - cc_ref[g])
            update(g, jnp.where(keep, s, -jnp.inf), vg)

        @pl.when(c == nk - 1)
        def _():
            for g in range(G):
                o_ref[g] = (acc_sc[g] / l_sc[g]).astype(o_ref.dtype)

    return pl.pallas_call(
        body,
        grid=(B, nk),
        in_specs=[
            pl.BlockSpec((None, G, R, HEAD_DIM), lambda b, c: (b, 0, 0, 0)),
            pl.BlockSpec((None, G, R, HEAD_DIM), lambda b, c: (b, 0, 0, 0)),
            pl.BlockSpec((None, G, R, HEAD_DIM), lambda b, c: (b, 0, 0, 0)),
            pl.BlockSpec((None, tk, N_HEADS, HEAD_DIM), lambda b, c: (b, c, 0, 0)),
            pl.BlockSpec((None, tk, N_HEADS, HEAD_DIM), lambda b, c: (b, c, 0, 0)),
            pl.BlockSpec((None, G, R, 1), lambda b, c: (b, 0, 0, 0)),
            pl.BlockSpec((None, G, 1, C), lambda b, c: (b, 0, 0, c)),
            pl.BlockSpec((None, G, 1, R), lambda b, c: (b, 0, 0, 0)),
        ],
        out_specs=pl.BlockSpec((None, G, R, HEAD_DIM), lambda b, c: (b, 0, 0, 0)),
        out_shape=jax.ShapeDtypeStruct((B, G, R, HEAD_DIM), BF16),
        scratch_shapes=[pltpu.VMEM((G, R, 1), F32), pltpu.VMEM((G, R, 1), F32), pltpu.VMEM((G, R, HEAD_DIM), F32)],
        compiler_params=_cparams(("parallel", "arbitrary")),
        name="fox_sample",
    )(qg, kn, vn, cache_k, cache_v, cq, ck_cache, ck_new)


def _spatial(name, gv, u, ln_g, ln_b, ws, bs, *, lc, tm):
    M = gv.shape[0]
    n_chunks = tm // lc

    def body(gv_ref, u_ref, lg_ref, lb_ref, ws_ref, bs_ref, vb_ref, sp_ref):
        x = gv_ref[...]
        mu = jnp.mean(x, axis=-1, keepdims=True)
        xc = x - mu
        var = jnp.mean(xc * xc, axis=-1, keepdims=True)
        vb = xc * lax.rsqrt(var + LN_EPS) * lg_ref[...] + lb_ref[...]
        vb_ref[...] = vb
        row = lax.broadcasted_iota(I32, (lc, lc), 0)
        col = lax.broadcasted_iota(I32, (lc, lc), 1)
        for g in range(GM_GROUPS):
            w = jnp.where(col <= row, ws_ref[g], 0.0).astype(BF16)
            b = bs_ref[g]
            cs = slice(g * GM_GROUP_DIM, (g + 1) * GM_GROUP_DIM)
            for c in range(n_chunks):
                rs = slice(c * lc, (c + 1) * lc)
                mixed = jnp.dot(w, vb_ref[rs, cs].astype(BF16), preferred_element_type=F32) + b
                sp_ref[rs, cs] = (u_ref[rs, cs] * mixed).astype(sp_ref.dtype)

    return pl.pallas_call(
        body,
        grid=(M // tm,),
        in_specs=[pl.BlockSpec((tm, GM_WIDTH), lambda i: (i, 0)), pl.BlockSpec((tm, GM_WIDTH), lambda i: (i, 0)),
                  pl.BlockSpec((1, GM_WIDTH), lambda i: (0, 0)), pl.BlockSpec((1, GM_WIDTH), lambda i: (0, 0)),
                  pl.BlockSpec((GM_GROUPS, lc, lc), lambda i: (0, 0, 0)),
                  pl.BlockSpec((GM_GROUPS, lc, 1), lambda i: (0, 0, 0))],
        out_specs=[pl.BlockSpec((tm, GM_WIDTH), lambda i: (i, 0)), pl.BlockSpec((tm, GM_WIDTH), lambda i: (i, 0))],
        out_shape=[jax.ShapeDtypeStruct((M, GM_WIDTH), F32), jax.ShapeDtypeStruct((M, GM_WIDTH), BF16)],
        compiler_params=_cparams(("parallel",)),
        name=name,
    )(gv, u, ln_g, ln_b, ws, bs)


def _route(logits_t, bias, *, tt):
    E, M = logits_t.shape
    per_group = E // N_EXPERT_GROUPS
    neg = -jnp.inf

    def body(lg_ref, b_ref, idx_ref, gate_ref, oh_ref):
        scores = jax.nn.sigmoid(lg_ref[...])
        sel = scores + b_ref[...]
        sel3 = sel.reshape(N_EXPERT_GROUPS, per_group, tt)
        io_in = lax.broadcasted_iota(I32, sel3.shape, 1)
        m1 = jnp.max(sel3, axis=1, keepdims=True)
        first = jnp.min(jnp.where(sel3 == m1, io_in, per_group), axis=1, keepdims=True)
        m2 = jnp.max(jnp.where(io_in == first, neg, sel3), axis=1, keepdims=True)
        grp_score = (m1 + m2).reshape(N_EXPERT_GROUPS, tt)
        io_g = lax.broadcasted_iota(I32, grp_score.shape, 0)
        grp_sel = jnp.zeros(grp_score.shape, F32)
        work = grp_score
        for _ in range(TOPK_GROUPS):
            mx = jnp.max(work, axis=0, keepdims=True)
            fi = jnp.min(jnp.where(work == mx, io_g, N_EXPERT_GROUPS), axis=0, keepdims=True)
            pick = io_g == fi
            grp_sel = jnp.where(pick, 1.0, grp_sel)
            work = jnp.where(pick, neg, work)
        keep3 = jnp.broadcast_to(grp_sel.reshape(N_EXPERT_GROUPS, 1, tt), sel3.shape)
        work = jnp.where(keep3 > 0.5, sel3, neg).reshape(E, tt)
        io_e = lax.broadcasted_iota(I32, (E, tt), 0)
        onehot = jnp.zeros((E, tt), F32)
        wts = []
        for k in range(TOP_K):
            mx = jnp.max(work, axis=0, keepdims=True)
            fi = jnp.min(jnp.where(work == mx, io_e, E), axis=0, keepdims=True)
            pick = io_e == fi
            idx_ref[k:k + 1, :] = fi
            wts.append(jnp.sum(jnp.where(pick, scores, 0.0), axis=0, keepdims=True))
            onehot = jnp.where(pick, 1.0, onehot)
            work = jnp.where(pick, neg, work)
        total = wts[0]
        for k in range(1, TOP_K):
            total = total + wts[k]
        for k in range(TOP_K):
            gate_ref[k:k + 1, :] = wts[k] / total * ROUTED_SCALE
        oh_ref[...] = onehot

    return pl.pallas_call(
        body,
        grid=(M // tt,),
        in_specs=[pl.BlockSpec((E, tt), lambda i: (0, i)), pl.BlockSpec((E, 1), lambda i: (0, 0))],
        out_specs=[pl.BlockSpec((TOP_K, tt), lambda i: (0, i)), pl.BlockSpec((TOP_K, tt), lambda i: (0, i)),
                   pl.BlockSpec((E, tt), lambda i: (0, i))],
        out_shape=[jax.ShapeDtypeStruct((TOP_K, M), I32), jax.ShapeDtypeStruct((TOP_K, M), F32),
                   jax.ShapeDtypeStruct((E, M), F32)],
        compiler_params=_cparams(("parallel",)),
        name="moe_route",
    )(logits_t, bias)


def _slots(idx_t, pos_t, start, *, tt):
    K, M = idx_t.shape
    E = pos_t.shape[0]

    def body(idx_ref, pos_ref, st_ref, o_ref):
        base = pos_ref[...] - 1.0 + st_ref[...]
        io_e = lax.broadcasted_iota(I32, (E, tt), 0)
        for k in range(K):
            pick = io_e == idx_ref[k:k + 1, :]
            o_ref[k:k + 1, :] = jnp.sum(jnp.where(pick, base, 0.0), axis=0, keepdims=True).astype(I32)

    return pl.pallas_call(
        body,
        grid=(M // tt,),
        in_specs=[pl.BlockSpec((K, tt), lambda i: (0, i)), pl.BlockSpec((E, tt), lambda i: (0, i)),
                  pl.BlockSpec((E, 1), lambda i: (0, 0))],
        out_specs=pl.BlockSpec((K, tt), lambda i: (0, i)),
        out_shape=jax.ShapeDtypeStruct((K, M), I32),
        compiler_params=_cparams(("parallel",)),
        name="moe_slots",
    )(idx_t, pos_t, start)


def _expert_schedule(counts, tm, n_blocks):
    E = counts.shape[0]
    nblk = (counts + tm - 1) // tm
    pad_end = jnp.cumsum(nblk * tm)
    pad_start = pad_end - nblk * tm
    n_used = pad_end[E - 1] // tm
    has = nblk > 0
    gord = jnp.cumsum(has.astype(I32)) - 1
    n_groups = jnp.sum(has.astype(I32))
    gexp = jnp.nonzero(has, size=E, fill_value=0)[0].astype(I32)
    blk = jnp.minimum(jnp.arange(n_blocks, dtype=I32), n_used - 1)
    blk_exp = jnp.minimum(jnp.sum((pad_end[None, :] <= (blk * tm)[:, None]).astype(I32), axis=1), E - 1)
    g = gord[blk_exp]
    j = blk - pad_start[blk_exp] // tm
    n = jnp.maximum(nblk[blk_exp], 1)
    live = (jnp.arange(n_blocks, dtype=I32) < n_used) & (g + 1 < n_groups)
    base = EXP_NP * (g + 1)
    qlo = jnp.where(live, base + (j * EXP_NP) // n, 0).astype(I32)
    qhi = jnp.where(live, base + ((j + 1) * EXP_NP) // n, 0).astype(I32)
    meta = jnp.stack([n_used, EXP_NP * n_groups]).astype(I32)
    return pad_start, (g.astype(I32), qlo, qhi, gexp, meta)


def _weight_stream(mats, stage, wsem, gexp_ref, total):
    ch = mats[0][0].shape[1] // EXP_NP

    def copies(q, slot):
        e = gexp_ref[q // EXP_NP]
        r0 = pl.multiple_of((q % EXP_NP) * ch, ch)
        return [pltpu.make_async_copy(w.at[e, pl.ds(r0, ch), :], stage.at[slot, i], wsem.at[slot])
                for i, (w, _) in enumerate(mats)]

    def issue(q):
        for c in copies(q, q % EXP_NS):
            c.start()

    def convert(q, carry):
        slot = q % EXP_NS
        for c in copies(q, slot):
            c.wait()
        half = (q // EXP_NP) % 2
        r0 = pl.multiple_of((q % EXP_NP) * ch, ch)
        for i, (_, wb) in enumerate(mats):
            wb[half, pl.ds(r0, ch), :] = stage[slot, i].astype(BF16)

        @pl.when(q + EXP_NS < total)
        def _():
            issue(q + EXP_NS)
        return carry

    return issue, convert


def _expert_up(h, slot_tok, wg, wu, sched):
    grp, qlo, qhi, gexp, meta = sched
    M, D = h.shape
    S = slot_tok.shape[0]
    DE = wg.shape[2]
    nb = S // MOE_TM

    def body(grp_ref, qlo_ref, qhi_ref, gexp_ref, meta_ref, st_ref, h_ref, wg_ref, wu_ref, o_ref,
             xbuf, wgb, wub, stage, xsem, wsem):
        b = pl.program_id(0)
        nu = meta_ref[0]
        issue, convert = _weight_stream([(wg_ref, wgb), (wu_ref, wub)], stage, wsem, gexp_ref, meta_ref[1])

        def row_copy(tok, buf, r):
            return pltpu.make_async_copy(h_ref.at[pl.ds(tok, 1)], xbuf.at[buf, pl.ds(r, 1)], xsem.at[buf])

        def gather(blk, buf):
            for r in range(MOE_TM):
                row_copy(st_ref[blk * MOE_TM + r], buf, r).start()

        @pl.when(b == 0)
        def _():
            gather(0, 0)
            for q in range(EXP_NS):
                issue(q)
            lax.fori_loop(0, EXP_NP, convert, 0)

        @pl.when(b + 1 < nu)
        def _():
            gather(b + 1, (b + 1) % 2)

        @pl.when(b < nu)
        def _():
            lax.fori_loop(qlo_ref[b], qhi_ref[b], convert, 0)
            buf = b % 2
            half = grp_ref[b] % 2
            for _ in range(MOE_TM):
                row_copy(0, buf, 0).wait()
            x = xbuf[buf].astype(BF16)
            g = jnp.dot(x, wgb[half], preferred_element_type=F32)
            u = jnp.dot(x, wub[half], preferred_element_type=F32)
            o_ref[...] = (_silu(g) * u).astype(o_ref.dtype)

        @pl.when(b >= nu)
        def _():
            o_ref[...] = jnp.zeros(o_ref.shape, o_ref.dtype)

    return pl.pallas_call(
        body,
        grid_spec=pltpu.PrefetchScalarGridSpec(
            num_scalar_prefetch=6,
            grid=(nb,),
            in_specs=[pl.BlockSpec(memory_space=pl.ANY), pl.BlockSpec(memory_space=pl.ANY),
                      pl.BlockSpec(memory_space=pl.ANY)],
            out_specs=pl.BlockSpec((MOE_TM, DE), lambda b, *_: (b, 0)),
            scratch_shapes=[pltpu.VMEM((2, MOE_TM, D), F32), pltpu.VMEM((2, D, DE), BF16),
                            pltpu.VMEM((2, D, DE), BF16), pltpu.VMEM((EXP_NS, 2, D // EXP_NP, DE), F32),
                            pltpu.SemaphoreType.DMA((2,)), pltpu.SemaphoreType.DMA((EXP_NS,))],
        ),
        out_shape=jax.ShapeDtypeStruct((S, DE), BF16),
        compiler_params=_cparams(("arbitrary",), EXPERT_UP_VMEM),
        name="moe_expert_up",
    )(grp, qlo, qhi, gexp, meta, slot_tok, h, wg, wu)


def _expert_down(hid, wd, sched):
    grp, qlo, qhi, gexp, meta = sched
    S, DE = hid.shape
    D = wd.shape[2]
    nb = S // MOE_TM

    def body(grp_ref, qlo_ref, qhi_ref, gexp_ref, meta_ref, h_ref, wd_ref, o_ref, wdb, stage, wsem):
        b = pl.program_id(0)
        nu = meta_ref[0]
        issue, convert = _weight_stream([(wd_ref, wdb)], stage, wsem, gexp_ref, meta_ref[1])

        @pl.when(b == 0)
        def _():
            for q in range(EXP_NS):
                issue(q)
            lax.fori_loop(0, EXP_NP, convert, 0)

        @pl.when(b < nu)
        def _():
            lax.fori_loop(qlo_ref[b], qhi_ref[b], convert, 0)
            o_ref[...] = jnp.dot(h_ref[...], wdb[grp_ref[b] % 2], preferred_element_type=F32)

        @pl.when(b >= nu)
        def _():
            o_ref[...] = jnp.zeros(o_ref.shape, o_ref.dtype)

    return pl.pallas_call(
        body,
        grid_spec=pltpu.PrefetchScalarGridSpec(
            num_scalar_prefetch=5,
            grid=(nb,),
            in_specs=[pl.BlockSpec((MOE_TM, DE),
                                   lambda b, grp, qlo, qhi, gexp, meta: (jnp.minimum(b, meta[0] - 1), 0)),
                      pl.BlockSpec(memory_space=pl.ANY)],
            out_specs=pl.BlockSpec((MOE_TM, D), lambda b, *_: (b, 0)),
            scratch_shapes=[pltpu.VMEM((2, DE, D), BF16), pltpu.VMEM((EXP_NS, 1, DE // EXP_NP, D), F32),
                            pltpu.SemaphoreType.DMA((EXP_NS,))],
        ),
        out_shape=jax.ShapeDtypeStruct((S, D), F32),
        compiler_params=_cparams(("arbitrary",)),
        name="moe_expert_down",
    )(grp, qlo, qhi, gexp, meta, hid, wd)


def _combine(name, out_slots, slots, gates, xres, gate2, final_g):
    M, D = xres.shape
    tb = COMBINE_TB
    nsteps = M // tb

    def body(slots_ref, src_ref, g_ref, x_ref, g2_ref, fg_ref, o_ref, buf, sem):
        i = pl.program_id(0)

        def row_copy(s, b, k, r):
            return pltpu.make_async_copy(src_ref.at[pl.ds(s, 1)], buf.at[b, k, pl.ds(r, 1)], sem.at[b])

        def issue(blk, b):
            for r in range(tb):
                for k in range(TOP_K):
                    row_copy(slots_ref[(blk * tb + r) * TOP_K + k], b, k, r).start()

        @pl.when(i == 0)
        def _():
            issue(0, 0)

        @pl.when(i + 1 < nsteps)
        def _():
            issue(i + 1, (i + 1) % 2)

        b = i % 2

        for _ in range(tb * TOP_K):
            row_copy(0, b, 0, 0).wait()

        g = g_ref[...]
        routed = g[:, 0:1] * buf[b, 0]
        for k in range(1, TOP_K):
            routed = routed + g[:, k:k + 1] * buf[b, k]
        x2 = x_ref[...] + g2_ref[...] * routed
        y = x2 * lax.rsqrt(jnp.mean(x2 * x2, axis=-1, keepdims=True) + RMS_EPS)
        o_ref[...] = y * fg_ref[...]

    return pl.pallas_call(
        body,
        grid_spec=pltpu.PrefetchScalarGridSpec(
            num_scalar_prefetch=1,
            grid=(nsteps,),
            in_specs=[pl.BlockSpec(memory_space=pl.ANY),
                      pl.BlockSpec((tb, TOP_K), lambda i, s: (i, 0)),
                      pl.BlockSpec((tb, D), lambda i, s: (i, 0)),
                      (pl.BlockSpec((1, D), lambda i, s: (0, 0)) if gate2.shape[0] == 1
                       else pl.BlockSpec((tb, D), lambda i, s: (i, 0))),
                      pl.BlockSpec((1, D), lambda i, s: (0, 0))],
            out_specs=pl.BlockSpec((tb, D), lambda i, s: (i, 0)),
            scratch_shapes=[pltpu.VMEM((2, TOP_K, tb, D), F32), pltpu.SemaphoreType.DMA((2,))],
        ),
        out_shape=jax.ShapeDtypeStruct((M, D), F32),
        compiler_params=_cparams(("arbitrary",)),
        name=name,
    )(slots, out_slots, gates, xres, gate2, final_g)


def _mixer_sublayer(tag, x, mod, w, attend, *, tm, lc):
    M, D = x.shape
    h = _normmod(f"{tag}_norm1", x, w["norm1_g"], mod["scale1"], mod["shift1"], tm=NORM_TM, out_dtype=BF16)
    ident = lambda accs, aux: [accs[0]]
    both = lambda accs, aux: [accs[0], accs[0]]
    (q,) = _mm(f"{tag}_q", [h], [(0, w["w_q"])], [], ident, [BF16], tm=tm, tn=512)
    k32, k16 = _mm(f"{tag}_k", [h], [(0, w["w_k"])], [], both, [F32, BF16], tm=tm, tn=512)
    v32, v16 = _mm(f"{tag}_v", [h], [(0, w["w_v"])], [], both, [F32, BF16], tm=tm, tn=512)
    (logf_pad,) = _mm(f"{tag}_f", [h], [(0, w["w_f"])], [w["b_f"]],
                      lambda accs, aux: [_log_sigmoid(accs[0] + aux[0])], [F32], tm=tm, tn=LANES)
    logf = logf_pad[:, :N_HEADS]
    (u,) = _mm(f"{tag}_u", [h], [(0, w["w_u"])], [], lambda accs, aux: [_gelu(accs[0])], [F32], tm=tm, tn=512)
    (gv,) = _mm(f"{tag}_vb", [h], [(0, w["w_vb"])], [], lambda accs, aux: [_gelu(accs[0])], [F32], tm=tm, tn=512)
    sig = lambda accs, aux: [jax.nn.sigmoid(accs[0])]
    (ga,) = _mm(f"{tag}_ga", [h], [(0, w["w_ga"])], [], sig, [BF16], tm=tm, tn=512)
    (gb,) = _mm(f"{tag}_gb", [h], [(0, w["w_gb"])], [], sig, [BF16], tm=tm, tn=512)

    attn = attend(q, k16, v16, logf)
    vb, spatial = _spatial(f"{tag}_spatial", gv, u, w["gm_ln_g"], w["gm_ln_b"], w["gm_ws"][:, :lc, :lc],
                           w["gm_bs"][:, :lc, None], lc=lc, tm=tm)
    (merged,) = _mm(f"{tag}_merge", [attn, spatial], [(0, w["w_a"]), (1, w["w_b"])], [ga, gb],
                    lambda accs, aux: [aux[0] * accs[0] + aux[1] * accs[1]], [BF16], tm=tm, tn=512)
    (x1,) = _mm(f"{tag}_out", [merged], [(0, w["w_out"])], [x, mod["gate1"]],
                lambda accs, aux: [aux[0] + aux[1] * accs[0]], [F32], tm=tm, tn=512)
    return x1, k32, v32, logf, vb


def _pad_lanes(x):
    pad = (-x.shape[-1]) % LANES
    return jnp.pad(x, [(0, 0)] * (x.ndim - 1) + [(0, pad)]) if pad else x


def kernel(x_prompt, x_sample, cache_k, cache_v, cache_logf, c_prompt, c_sample, ada_w, ada_b, norm1_g, w_in, b_forget, gm_ln_g, gm_ln_b, gm_ws, gm_bs, w_branch_a, w_branch_b, w_out, norm2_g, w_router, router_bias, w_exp_gate, w_exp_up, w_exp_down, w_sh_gate, w_sh_up, w_sh_down, final_g):
    assert ada_w.shape[0] == 1, "single trunk layer"
    B, L, D = x_prompt.shape
    SB, ST, _ = x_sample.shape
    assert B == 1
    P = cache_k.shape[2]
    Mp, Ms = B * L, SB * ST
    M = Mp + Ms

    wi = w_in[0]
    c_q, c_k, c_v, c_f = 0, WIDTH_A, 2 * WIDTH_A, 3 * WIDTH_A
    c_u = c_f + N_HEADS
    c_vb = c_u + GM_WIDTH
    c_ga = c_vb + GM_WIDTH
    c_gb = c_ga + D
    w = {
        "norm1_g": norm1_g, "gm_ln_g": gm_ln_g, "gm_ln_b": gm_ln_b, "gm_ws": gm_ws[0], "gm_bs": gm_bs[0],
        "w_q": wi[:, c_q:c_k].astype(BF16), "w_k": wi[:, c_k:c_v].astype(BF16), "w_v": wi[:, c_v:c_f].astype(BF16),
        "w_f": _pad_lanes(wi[:, c_f:c_u]).astype(BF16), "b_f": _pad_lanes(b_forget),
        "w_u": wi[:, c_u:c_vb].astype(BF16), "w_vb": wi[:, c_vb:c_ga].astype(BF16),
        "w_ga": wi[:, c_ga:c_gb].astype(BF16), "w_gb": wi[:, c_gb:].astype(BF16),
        "w_a": w_branch_a[0].astype(BF16), "w_b": w_branch_b[0].astype(BF16), "w_out": w_out[0].astype(BF16),
    }

    c_all = jnp.concatenate([c_prompt, c_sample], axis=0)
    n_c = c_all.shape[0]
    c_all = jnp.pad(c_all, ((0, (-n_c) % 32), (0, 0)))
    (mod,) = _mm("ada_mod", [c_all], [(0, ada_w[0])], [ada_b], lambda accs, aux: [accs[0] + aux[0]], [F32],
                 tm=c_all.shape[0], tn=512, x_pre=_silu)
    names = ["shift1", "scale1", "gate1", "shift2", "scale2", "gate2"]
    mod_p = {n: mod[0:B, i * D:(i + 1) * D] for i, n in enumerate(names)}
    mod_s = {n: jnp.broadcast_to(mod[B:B + SB, None, i * D:(i + 1) * D], (SB, ST, D)).reshape(Ms, D)
             for i, n in enumerate(names)}

    def attend_prompt(q, k16, v16, logf):
        cum_t = _cumsum_lanes("p_cumsum", jnp.transpose(logf)[None])[0]
        qa, ka = _fox_augment(q, k16, cum_t[:, :, None], tr=512)
        return _fox_prompt2(qa, ka, v16, tb=1024, half=512, rc=32)

    def attend_sample(q, k16, v16, logf):
        lf = jnp.concatenate([cache_logf[0].astype(F32), logf.reshape(SB, ST, N_HEADS)], axis=1)
        lf_t = _pad_lanes(jnp.transpose(lf, (0, 2, 1)))
        cum_t = _cumsum_lanes("s_cumsum", lf_t.reshape(1, SB * N_HEADS, -1)).reshape(lf_t.shape)
        G = N_HEADS // SUBLANES

        def group_rows(x):
            x = jnp.transpose(x.reshape(SB, ST, G, SUBLANES, HEAD_DIM), (0, 2, 3, 1, 4))
            return x.reshape(SB, G, SUBLANES * ST, HEAD_DIM)

        cum_g = cum_t[:, :, :P + ST].reshape(SB, G, SUBLANES, P + ST)
        cq = cum_g[:, :, :, P:].reshape(SB, G, SUBLANES * ST, 1)
        ck_new = cum_g[:, :, :, P:].reshape(SB, G, 1, SUBLANES * ST)
        ck_cache = jnp.transpose(cum_g[:, :, :, :P], (0, 1, 3, 2)).reshape(SB, G, 1, P * SUBLANES)
        o = _fox_sample(group_rows(q), group_rows(k16), group_rows(v16), cache_k[0], cache_v[0],
                        cq, ck_cache, ck_new, tk=256)
        o = jnp.transpose(o.reshape(SB, G, SUBLANES, ST, HEAD_DIM), (0, 3, 1, 2, 4))
        return o.reshape(Ms, WIDTH_A)

    x1p, kp, vp, fp, _ = _mixer_sublayer("p", x_prompt.reshape(Mp, D), mod_p, w, attend_prompt, tm=512,
                                         lc=min(L, GM_CHUNK))
    x1s, ks, vs, fs, gs = _mixer_sublayer("s", x_sample.reshape(Ms, D), mod_s, w, attend_sample, tm=512,
                                          lc=min(ST, GM_CHUNK))

    wr = jnp.transpose(w_router[0])
    wr_hi = wr.astype(BF16)
    wr_lo = (wr - wr_hi.astype(F32)).astype(BF16)
    h2p, lgp = _normmod_router("p_norm2", x1p, norm2_g, mod_p["scale2"], mod_p["shift2"], wr_hi, wr_lo, tm=NORM_TM)
    h2s, lgs = _normmod_router("s_norm2", x1s, norm2_g, mod_s["scale2"], mod_s["shift2"], wr_hi, wr_lo, tm=NORM_TM)
    h2 = jnp.concatenate([h2p, h2s], axis=0)
    logits_t = jnp.concatenate([lgp, lgs], axis=1)

    idx_t, gates_t, onehot_t = _route(logits_t, jnp.transpose(router_bias).astype(F32), tt=512)
    pos_t = _cumsum_lanes("moe_rank", onehot_t[None])[0]
    counts = pos_t[:, M - 1].astype(I32)
    n_blocks = -(-(M * TOP_K + N_EXPERTS * (MOE_TM - 1)) // MOE_TM)
    n_slots = n_blocks * MOE_TM
    pad_start, sched = _expert_schedule(counts, MOE_TM, n_blocks)
    slot_t = _slots(idx_t, pos_t, pad_start.astype(F32)[:, None], tt=512)
    slots = jnp.transpose(slot_t).reshape(M * TOP_K)
    gates = jnp.transpose(gates_t)
    slot_tok = jnp.zeros((n_slots,), I32).at[slots].set(jnp.arange(M * TOP_K, dtype=I32) // TOP_K)

    hid = _expert_up(h2, slot_tok, w_exp_gate[0], w_exp_up[0], sched)
    out_slots = _expert_down(hid, w_exp_down[0], sched)

    wsg, wsu, wsd = w_sh_gate[0].astype(BF16), w_sh_up[0].astype(BF16), w_sh_down[0].astype(BF16)
    swi = lambda accs, aux: [_silu(accs[0]) * accs[1]]
    resid = lambda accs, aux: [aux[0] + aux[1] * accs[0]]

    def finish(tag, h2g, x1g, mod_g, slots_g, gates_g):
        (hs,) = _mm(f"{tag}_sh_up", [h2g], [(0, wsg), (0, wsu)], [], swi, [BF16], tm=512, tn=512)
        (xres,) = _mm(f"{tag}_sh_down", [hs], [(0, wsd)], [x1g, mod_g["gate2"]], resid, [F32], tm=512, tn=512)
        return _combine(f"{tag}_combine", out_slots, slots_g, gates_g, xres, mod_g["gate2"], final_g[None, :])

    y_p = finish("p", h2p, x1p, mod_p, slots[:Mp * TOP_K], gates[:Mp])
    y_s = finish("s", h2s, x1s, mod_s, slots[Mp * TOP_K:], gates[Mp:])

    hd = (N_HEADS, HEAD_DIM)
    return (y_p.reshape(B, L, D), y_s.reshape(SB, ST, D),
            kp.reshape(1, B, L, *hd), vp.reshape(1, B, L, *hd), fp.reshape(1, B, L, N_HEADS),
            ks.reshape(1, SB, ST, *hd), vs.reshape(1, SB, ST, *hd), fs.reshape(1, SB, ST, N_HEADS),
            gs.reshape(1, SB, ST, GM_WIDTH))
```
